```python
import jax, jax.numpy as jnp
from jax import lax
import numpy as np

D_MODEL = 1024
BATCH = 8
SEQ = 2048
DEPTH = 1
DEC_BATCH = 128
DEC_SEQ = 8
PAST_LEN = 2048
PAGE_SIZE = 128

N_META = 16
H_RET = D_MODEL // 256
DK_RET = 256
DV_RET = 512
RET_CHUNK = 128
ROPE_BASE = 10000.0
N_HEADS = D_MODEL // 128
N_KV_HEADS = 2
D_HEAD = 128
N_IDX_HEADS = 8
D_IDX = 64
INDEX_TOPK = 256
Q_BLOCK = 128
D_FF = -(-(8 * D_MODEL) // (3 * 256)) * 256
NORM_EPS = 1e-6

RET_QK_W = H_RET * DK_RET
RET_V_W = H_RET * DV_RET
DSA_Q_W = N_HEADS * D_HEAD
DSA_KV_W = N_KV_HEADS * D_HEAD
IDX_Q_W = N_IDX_HEADS * D_IDX
IN_COLS = 2 * RET_QK_W + 2 * RET_V_W + DSA_Q_W + 2 * DSA_KV_W + IDX_Q_W + D_IDX + N_IDX_HEADS + 2 * D_MODEL
IDX_SCALE = (N_IDX_HEADS ** -0.5) * (D_IDX ** -0.5)

kernel_name = "retention_dsa_gated_hybrid_step"


def rms_norm(x, w=None):
    xf = x.astype(jnp.float32)
    y = (xf * lax.rsqrt(jnp.mean(xf * xf, axis=-1, keepdims=True) + NORM_EPS)).astype(x.dtype)
    return y if w is None else y * w


def layer_norm(x, w, b):
    xf = x.astype(jnp.float32)
    mu = jnp.mean(xf, axis=-1, keepdims=True)
    var = jnp.mean(jnp.square(xf - mu), axis=-1, keepdims=True)
    return ((xf - mu) * lax.rsqrt(var + NORM_EPS)).astype(x.dtype) * w + b


def rotary(x, pos):
    half = x.shape[-1] // 2
    inv = ROPE_BASE ** (-jnp.arange(half, dtype=jnp.float32) / half)
    ang = pos.astype(jnp.float32)[:, None] * inv[None, :]
    cos = jnp.cos(ang)[None, :, None, :]
    sin = jnp.sin(ang)[None, :, None, :]
    x1 = x[..., :half].astype(jnp.float32)
    x2 = x[..., half:].astype(jnp.float32)
    return jnp.concatenate([x1 * cos - x2 * sin, x1 * sin + x2 * cos], axis=-1).astype(x.dtype)


def retention_log_gamma():
    return jnp.log1p(-jnp.exp2(-5.0 - jnp.arange(H_RET, dtype=jnp.float32)))


def retention_chunk(S, q, k, v):
    C = q.shape[2]
    lg = retention_log_gamma()
    i = jnp.arange(C, dtype=jnp.float32)
    diff = i[:, None] - i[None, :]
    decay = jnp.where(diff >= 0, jnp.exp(lg[:, None, None] * jnp.maximum(diff, 0.0)), 0.0).astype(q.dtype)
    q_decay = jnp.exp(lg[:, None] * (i + 1.0)).astype(q.dtype)
    k_decay = jnp.exp(lg[:, None] * (C - 1.0 - i)).astype(q.dtype)
    s_decay = jnp.exp(lg * C).astype(q.dtype)
    scores = jnp.einsum("bhid,bhjd->bhij", q, k) * decay
    o = (jnp.einsum("bhij,bhjv->bhiv", scores, v)
         + jnp.einsum("bhid,bhdv->bhiv", q, S) * q_decay[None, :, :, None])
    S_new = (S * s_decay[None, :, None, None]
             + jnp.einsum("bhjd,bhjv->bhdv", k * k_decay[None, :, :, None], v))
    return S_new, o


def retention_prompt(q, k, v):
    B, H, L, _ = q.shape
    S0 = jnp.zeros((B, H, DK_RET, DV_RET), v.dtype)
    S1, o_meta = retention_chunk(S0, q[:, :, :N_META], k[:, :, :N_META], v[:, :, :N_META])
    n_c = (L - N_META) // RET_CHUNK

    def to_chunks(a):
        return a[:, :, N_META:].reshape(B, H, n_c, RET_CHUNK, a.shape[-1]).transpose(2, 0, 1, 3, 4)

    S_fin, o_c = lax.scan(lambda S, qkv: retention_chunk(S, *qkv), S1,
                          (to_chunks(q), to_chunks(k), to_chunks(v)))
    o_c = o_c.transpose(1, 2, 0, 3, 4).reshape(B, H, n_c * RET_CHUNK, DV_RET)
    return jnp.concatenate([o_meta, o_c], axis=2), S_fin


def indexer_scores(qi, wi, ki):
    s = jnp.einsum("bqhd,bsd->bqhs", qi.astype(jnp.float32), ki.astype(jnp.float32))
    return jnp.einsum("bqhs,bqh->bqs", jax.nn.relu(s), wi.astype(jnp.float32) * IDX_SCALE)


def sparse_attend(q, kg, vg, valid):
    B, Q = q.shape[:2]
    qg = q.reshape(B, Q, N_KV_HEADS, N_HEADS // N_KV_HEADS, D_HEAD)
    s = jnp.einsum("bqngd,bqknd->bqngk", qg.astype(jnp.float32), kg.astype(jnp.float32)) * (D_HEAD ** -0.5)
    s = jnp.where(valid[:, :, None, None, :], s, -jnp.inf)
    p = jax.nn.softmax(s, axis=-1).astype(vg.dtype)
    o = jnp.einsum("bqngk,bqknd->bqngd", p, vg)
    return o.reshape(B, Q, N_HEADS, D_HEAD)


def dsa_prompt(q, k, v, qi, ki, wi, n_real):
    B, L = q.shape[:2]
    topk = min(INDEX_TOPK, n_real // 4)
    nb = -(-L // Q_BLOCK)
    Lp = nb * Q_BLOCK

    def blocks(a):
        a = jnp.pad(a, [(0, 0), (0, Lp - L)] + [(0, 0)] * (a.ndim - 2))
        return jnp.swapaxes(a.reshape((B, nb, Q_BLOCK) + a.shape[2:]), 0, 1)

    q_pos = jnp.arange(Lp).reshape(nb, Q_BLOCK)
    key_pos = jnp.arange(L)
    bidx = jnp.arange(B)[:, None, None]

    def one_block(args):
        qb, qib, wib, pb = args
        score = indexer_scores(qib, wib, ki)
        score = jnp.where(key_pos[None, None, :] <= pb[None, :, None], score, -jnp.inf)
        _, sel = lax.top_k(score, topk)
        valid = sel <= pb[None, :, None]
        return sparse_attend(qb, k[bidx, sel], v[bidx, sel], valid)

    o = lax.map(one_block, (blocks(q), blocks(qi), blocks(wi), q_pos))
    return jnp.swapaxes(o, 0, 1).reshape(B, Lp, N_HEADS, D_HEAD)[:, :L]


def dsa_sample(q, k_new, v_new, qi, ki_new, wi, cache_k, cache_v, cache_kidx, page_table):
    DB, Q = q.shape[:2]
    n_pages = page_table.shape[1]
    past = n_pages * PAGE_SIZE
    L = past + Q
    topk = min(INDEX_TOPK, L // 4)
    ki_all = jnp.concatenate([cache_kidx[page_table].reshape(DB, past, D_IDX), ki_new], axis=1)
    q_pos = past + jnp.arange(Q)
    key_pos = jnp.arange(L)
    score = indexer_scores(qi, wi, ki_all)
    score = jnp.where(key_pos[None, None, :] <= q_pos[None, :, None], score, -jnp.inf)
    _, sel = lax.top_k(score, topk)
    valid = sel <= q_pos[None, :, None]
    bidx = jnp.arange(DB)[:, None, None]
    in_past = (sel < past)[..., None, None]
    ps = jnp.minimum(sel, past - 1)
    phys = page_table[bidx, ps // PAGE_SIZE]
    off = ps % PAGE_SIZE
    ns = jnp.clip(sel - past, 0, Q - 1)
    kg = jnp.where(in_past, cache_k[phys, off], k_new[bidx, ns])
    vg = jnp.where(in_past, cache_v[phys, off], v_new[bidx, ns])
    return sparse_attend(q, kg, vg, valid)


def mixer_inputs(h, pos, norm_mix_w, w_in, dsa_q_norm_w, dsa_k_norm_w, idx_k_norm_w, idx_k_norm_b):
    B, L = h.shape[:2]
    z = rms_norm(h, norm_mix_w) @ w_in
    sizes = (RET_QK_W, RET_QK_W, RET_V_W, RET_V_W, DSA_Q_W, DSA_KV_W, DSA_KV_W,
             IDX_Q_W, D_IDX, N_IDX_HEADS, 2 * D_MODEL)
    rq, rk, rv, rg, aq, ak, av, iq, ik, iw, gz = jnp.split(z, [int(c) for c in np.cumsum(sizes)[:-1]], axis=-1)
    rq = rotary(rq.reshape(B, L, H_RET, DK_RET), pos).transpose(0, 2, 1, 3)
    rk = (rotary(rk.reshape(B, L, H_RET, DK_RET), pos) * (DK_RET ** -0.5)).transpose(0, 2, 1, 3)
    rv = rv.reshape(B, L, H_RET, DV_RET).transpose(0, 2, 1, 3)
    aq = rms_norm(aq.reshape(B, L, N_HEADS, D_HEAD), dsa_q_norm_w)
    ak = rms_norm(ak.reshape(B, L, N_KV_HEADS, D_HEAD), dsa_k_norm_w)
    av = av.reshape(B, L, N_KV_HEADS, D_HEAD)
    iq = iq.reshape(B, L, N_IDX_HEADS, D_IDX)
    ik = layer_norm(ik, idx_k_norm_w, idx_k_norm_b)
    return rq, rk, rv, rg, aq, ak, av, iq, ik, iw, jax.nn.sigmoid(gz)


def merge_and_ffn(h, o_ret, rg, o_dsa, gates, w_ret_proj, w_dsa_proj, w_out, norm_ffn_w, w_ffn_in, w_ffn_out):
    B, L = h.shape[:2]
    o_ret = rms_norm(o_ret.transpose(0, 2, 1, 3)).reshape(B, L, RET_V_W) * jax.nn.silu(rg)
    br_ret = o_ret @ w_ret_proj
    br_dsa = o_dsa.reshape(B, L, DSA_Q_W) @ w_dsa_proj
    g_ret, g_dsa = jnp.split(gates, 2, axis=-1)
    h = h + (g_ret * br_ret + g_dsa * br_dsa) @ w_out
    a, b = jnp.split(rms_norm(h, norm_ffn_w) @ w_ffn_in, 2, axis=-1)
    return h + (jax.nn.silu(a) * b) @ w_ffn_out


def setup_inputs(seed: int = 0) -> dict:
    key = jax.random.key(seed)
    ks = jax.random.split(key, 20)
    f32 = jnp.float32
    n_pages = PAST_LEN // PAGE_SIZE
    n_used = DEC_BATCH * n_pages
    n_pool = (n_used * 5) // 4

    def nrm(k, shape, scale=1.0):
        return jax.random.normal(k, shape, f32) * scale

    def gain(k, shape):
        return 1.0 + 0.02 * jax.random.normal(k, shape, f32)

    return {
        "x_prompt": nrm(ks[0], (BATCH, SEQ, D_MODEL)),
        "x_sample": nrm(ks[1], (DEC_BATCH, DEC_SEQ, D_MODEL)),
        "cache_k": nrm(ks[2], (DEPTH, n_pool, PAGE_SIZE, N_KV_HEADS, D_HEAD)),
        "cache_v": nrm(ks[3], (DEPTH, n_pool, PAGE_SIZE, N_KV_HEADS, D_HEAD)),
        "cache_kidx": nrm(ks[4], (DEPTH, n_pool, PAGE_SIZE, D_IDX)),
        "state_ret": nrm(ks[5], (DEPTH, DEC_BATCH, H_RET, DK_RET, DV_RET), 0.5),
        "page_table": jax.random.permutation(ks[6], n_pool)[:n_used].reshape(DEC_BATCH, n_pages).astype(jnp.int32),
        "meta_tokens": nrm(ks[7], (N_META, D_MODEL)),
        "norm_mix_w": gain(ks[8], (DEPTH, D_MODEL)),
        "w_in": nrm(ks[9], (DEPTH, D_MODEL, IN_COLS), D_MODEL ** -0.5),
        "w_ret_proj": nrm(ks[10], (DEPTH, RET_V_W, D_MODEL), RET_V_W ** -0.5),
        "dsa_q_norm_w": gain(ks[11], (DEPTH, D_HEAD)),
        "dsa_k_norm_w": gain(ks[12], (DEPTH, D_HEAD)),
        "idx_k_norm_w": gain(ks[13], (DEPTH, D_IDX)),
        "idx_k_norm_b": nrm(ks[14], (DEPTH, D_IDX), 0.02),
        "w_dsa_proj": nrm(ks[15], (DEPTH, DSA_Q_W, D_MODEL), DSA_Q_W ** -0.5),
        "w_out": nrm(ks[16], (DEPTH, D_MODEL, D_MODEL), D_MODEL ** -0.5),
        "norm_ffn_w": gain(ks[17], (DEPTH, D_MODEL)),
        "w_ffn_in": nrm(ks[18], (DEPTH, D_MODEL, 2 * D_FF), D_MODEL ** -0.5),
        "w_ffn_out": nrm(ks[19], (DEPTH, D_FF, D_MODEL), D_FF ** -0.5),
    }


def reference(x_prompt, x_sample, cache_k, cache_v, cache_kidx, state_ret, page_table,
              meta_tokens, norm_mix_w, w_in, w_ret_proj, dsa_q_norm_w, dsa_k_norm_w,
              idx_k_norm_w, idx_k_norm_b, w_dsa_proj, w_out, norm_ffn_w, w_ffn_in, w_ffn_out):
    B, n_real = x_prompt.shape[:2]
    h_p = jnp.concatenate(
        [jnp.broadcast_to(meta_tokens.astype(x_prompt.dtype)[None], (B, N_META, D_MODEL)), x_prompt], axis=1)
    h_s = x_sample
    past = page_table.shape[1] * PAGE_SIZE
    pos_p = jnp.arange(h_p.shape[1])
    pos_s = past + jnp.arange(x_sample.shape[1])
    kp, vp, kip, sp, ks_, vs_, kis, ss = [], [], [], [], [], [], [], []
    for l in range(DEPTH):
        proj = (norm_mix_w[l], w_in[l], dsa_q_norm_w[l], dsa_k_norm_w[l], idx_k_norm_w[l], idx_k_norm_b[l])
        post = (w_ret_proj[l], w_dsa_proj[l], w_out[l], norm_ffn_w[l], w_ffn_in[l], w_ffn_out[l])
        rq, rk, rv, rg, aq, ak, av, iq, ik, iw, gates = mixer_inputs(h_p, pos_p, *proj)
        o_ret, S_p = retention_prompt(rq, rk, rv)
        o_dsa = dsa_prompt(aq, ak, av, iq, ik, iw, n_real)
        kp.append(ak); vp.append(av); kip.append(ik); sp.append(S_p)
        h_p = merge_and_ffn(h_p, o_ret, rg, o_dsa, gates, *post)
        rq, rk, rv, rg, aq, ak, av, iq, ik, iw, gates = mixer_inputs(h_s, pos_s, *proj)
        S_s, o_ret = retention_chunk(state_ret[l], rq, rk, rv)
        o_dsa = dsa_sample(aq, ak, av, iq, ik, iw, cache_k[l], cache_v[l], cache_kidx[l], page_table)
        ks_.append(ak); vs_.append(av); kis.append(ik); ss.append(S_s)
        h_s = merge_and_ffn(h_s, o_ret, rg, o_dsa, gates, *post)
    y_prompt = h_p[:, N_META:]
    return (y_prompt, h_s, jnp.stack(kp), jnp.stack(vp), jnp.stack(kip), jnp.stack(sp),
            jnp.stack(ks_), jnp.stack(vs_), jnp.stack(kis), jnp.stack(ss))
```

```python
import functools

import numpy as np
import jax
import jax.numpy as jnp
from jax import lax
from jax.experimental import pallas as pl
from jax.experimental.pallas import tpu as pltpu

F32 = jnp.float32
BF16 = jnp.bfloat16
I32 = jnp.int32

N_META = 16
DK_RET = 256
DV_RET = 512
RET_CHUNK = 128
ROPE_BASE = 10000.0
D_HEAD = 128
N_KV_HEADS = 2
N_IDX_HEADS = 8
D_IDX = 64
INDEX_TOPK = 256
Q_BLOCK = 128
PAGE_SIZE = 128
NORM_EPS = 1e-6
IDX_SCALE = (N_IDX_HEADS ** -0.5) * (D_IDX ** -0.5)

LANES = 128
VMEM_LIMIT = 56 * 1024 * 1024

KEY_NEG_INF = int(np.array(0xFF800000 ^ 0x7FFFFFFF, np.uint32).view(np.int32))
POS_PAD = 1 << 30


def _cparams(sem):
    return pltpu.CompilerParams(dimension_semantics=sem, vmem_limit_bytes=VMEM_LIMIT)


def _dot(a, b):
    return jnp.dot(a, b, preferred_element_type=F32)


def _dot_nt(a, b):
    return lax.dot_general(a, b, (((1,), (1,)), ((), ())), preferred_element_type=F32)


def _dot_tn(a, b):
    return lax.dot_general(a, b, (((0,), (0,)), ((), ())), preferred_element_type=F32)


def _rms(x):
    return x * lax.rsqrt(jnp.mean(x * x, axis=-1, keepdims=True) + NORM_EPS)


def _norm_body(x_ref, w_ref, o_ref):
    o_ref[...] = (_rms(x_ref[...]) * w_ref[...]).astype(o_ref.dtype)


def _rmsnorm_bf16(x, w, tm):
    t, d = x.shape
    return pl.pallas_call(
        _norm_body,
        grid=(t // tm,),
        in_specs=[pl.BlockSpec((tm, d), lambda i: (i, 0)), pl.BlockSpec((1, d), lambda i: (0, 0))],
        out_specs=pl.BlockSpec((tm, d), lambda i: (i, 0)),
        out_shape=jax.ShapeDtypeStruct((t, d), BF16),
        compiler_params=_cparams(("parallel",)),
        name="rmsnorm",
    )(x, w.reshape(1, d))


def _proj_rotary_body(x_ref, w_ref, cos_ref, sin_ref, o_ref, *, q_blocks):
    z = _dot(x_ref[...], w_ref[...])
    scale = jnp.where(pl.program_id(0) >= q_blocks, DK_RET ** -0.5, 1.0).astype(F32)
    cos = cos_ref[...]
    sin = sin_ref[...]
    half = DK_RET // 2
    for h in range(z.shape[1] // DK_RET):
        x1 = z[:, h * DK_RET:h * DK_RET + half]
        x2 = z[:, h * DK_RET + half:(h + 1) * DK_RET]
        o_ref[:, h * DK_RET:h * DK_RET + half] = ((x1 * cos - x2 * sin) * scale).astype(o_ref.dtype)
        o_ref[:, h * DK_RET + half:(h + 1) * DK_RET] = ((x1 * sin + x2 * cos) * scale).astype(o_ref.dtype)


def _proj_rotary(xn, w, cos, sin, tm, tn):
    t, d = xn.shape
    n = w.shape[1]
    pos_blocks = cos.shape[0] // tm
    return pl.pallas_call(
        functools.partial(_proj_rotary_body, q_blocks=(n // 2) // tn),
        grid=(n // tn, t // tm),
        in_specs=[pl.BlockSpec((tm, d), lambda j, i: (i, 0)),
                  pl.BlockSpec((d, tn), lambda j, i: (0, j)),
                  pl.BlockSpec((tm, DK_RET // 2), lambda j, i: (i % pos_blocks, 0)),
                  pl.BlockSpec((tm, DK_RET // 2), lambda j, i: (i % pos_blocks, 0))],
        out_specs=pl.BlockSpec((tm, tn), lambda j, i: (i, j)),
        out_shape=jax.ShapeDtypeStruct((t, n), BF16),
        compiler_params=_cparams(("parallel", "parallel")),
        name="proj_rotary",
    )(xn, w, cos, sin)


def _proj_plain_body(x_ref, w_ref, o_ref):
    o_ref[...] = _dot(x_ref[...], w_ref[...]).astype(o_ref.dtype)


def _proj_sigmoid_body(x_ref, w_ref, o_ref):
    z = _dot(x_ref[...], w_ref[...])
    o_ref[...] = (1.0 / (1.0 + jnp.exp(-z))).astype(o_ref.dtype)


def _proj_headnorm_body(x_ref, w_ref, g_ref, o_ref):
    z = _dot(x_ref[...], w_ref[...])
    g = g_ref[...] * (D_HEAD ** -0.5)
    for h in range(z.shape[1] // D_HEAD):
        o_ref[:, h * D_HEAD:(h + 1) * D_HEAD] = (_rms(z[:, h * D_HEAD:(h + 1) * D_HEAD]) * g).astype(o_ref.dtype)


def _proj_simple(body, xn, w, tm, tn, name, extra=()):
    t, d = xn.shape
    n = w.shape[1]
    extra_specs = [pl.BlockSpec(e.shape, lambda j, i: (0, 0)) for e in extra]
    return pl.pallas_call(
        body,
        grid=(n // tn, t // tm),
        in_specs=[pl.BlockSpec((tm, d), lambda j, i: (i, 0)),
                  pl.BlockSpec((d, tn), lambda j, i: (0, j))] + extra_specs,
        out_specs=pl.BlockSpec((tm, tn), lambda j, i: (i, j)),
        out_shape=jax.ShapeDtypeStruct((t, n), BF16),
        compiler_params=_cparams(("parallel", "parallel")),
        name=name,
    )(xn, w, *extra)


_KV_W = N_KV_HEADS * D_HEAD
_IQ_W = N_IDX_HEADS * D_IDX
_S_K = 0
_S_V = _KV_W
_S_IQ = 2 * _KV_W
_S_IK = _S_IQ + _IQ_W
_S_IW = _S_IK + LANES
_S_END = _S_IW + LANES


def _proj_small_body(x_ref, w_ref, kg_ref, lnw_ref, lnb_ref,
                     k_ref, v_ref, kb_ref, vb_ref, iq_ref, ik_ref, ikb_ref, iw_ref):
    z = _dot(x_ref[...], w_ref[...])
    kg = kg_ref[...]
    for h in range(N_KV_HEADS):
        kh = _rms(z[:, _S_K + h * D_HEAD:_S_K + (h + 1) * D_HEAD]) * kg
        k_ref[:, h * D_HEAD:(h + 1) * D_HEAD] = kh
        kb_ref[:, h * D_HEAD:(h + 1) * D_HEAD] = kh.astype(BF16)
    v = z[:, _S_V:_S_V + _KV_W]
    v_ref[...] = v
    vb_ref[...] = v.astype(BF16)
    for h in range(N_IDX_HEADS):
        iq_ref[h] = z[:, _S_IQ + h * D_IDX:_S_IQ + (h + 1) * D_IDX].astype(BF16)
    ik = z[:, _S_IK:_S_IK + D_IDX]
    mu = jnp.mean(ik, axis=-1, keepdims=True)
    var = jnp.mean(jnp.square(ik - mu), axis=-1, keepdims=True)
    ikn = (ik - mu) * lax.rsqrt(var + NORM_EPS) * lnw_ref[...] + lnb_ref[...]
    ik_ref[...] = ikn
    ikb_ref[...] = ikn.astype(BF16)
    iw_ref[...] = z[:, _S_IW:_S_IW + N_IDX_HEADS] * IDX_SCALE


def _proj_small(xn, w, k_gain, ln_w, ln_b, tm):
    t, d = xn.shape
    row = lambda width: pl.BlockSpec((tm, width), lambda i: (i, 0))
    full = lambda a: pl.BlockSpec(a.shape, lambda i: (0, 0))
    return pl.pallas_call(
        _proj_small_body,
        grid=(t // tm,),
        in_specs=[row(d), full(w), full(k_gain), full(ln_w), full(ln_b)],
        out_specs=[row(_KV_W), row(_KV_W), row(_KV_W), row(_KV_W),
                   pl.BlockSpec((N_IDX_HEADS, tm, D_IDX), lambda i: (0, i, 0)),
                   row(D_IDX), row(D_IDX), row(N_IDX_HEADS)],
        out_shape=[jax.ShapeDtypeStruct((t, _KV_W), F32), jax.ShapeDtypeStruct((t, _KV_W), F32),
                   jax.ShapeDtypeStruct((t, _KV_W), BF16), jax.ShapeDtypeStruct((t, _KV_W), BF16),
                   jax.ShapeDtypeStruct((N_IDX_HEADS, t, D_IDX), BF16),
                   jax.ShapeDtypeStruct((t, D_IDX), F32), jax.ShapeDtypeStruct((t, D_IDX), BF16),
                   jax.ShapeDtypeStruct((t, N_IDX_HEADS), F32)],
        compiler_params=_cparams(("parallel",)),
        name="proj_small",
    )(xn, w, k_gain, ln_w, ln_b)


def _retention_body(q_ref, k_ref, v_ref, s_in_ref, dec_ref, qd_ref, kd_ref, sd_ref,
                    o_ref, s_out_ref, s_scr, *, hb):
    c = pl.program_id(2)

    @pl.when(c == 0)
    def _():
        s_scr[...] = s_in_ref[0]

    for h in range(hb):
        q = q_ref[0, :, h * DK_RET:(h + 1) * DK_RET]
        k = k_ref[0, :, h * DK_RET:(h + 1) * DK_RET]
        v = v_ref[0, :, h * DV_RET:(h + 1) * DV_RET]
        s = s_scr[h]
        scores = _dot_nt(q, k) * dec_ref[h]
        o = _dot(scores.astype(BF16), v) + _dot(q, s.astype(BF16)) * qd_ref[h]
        kd = (k.astype(F32) * kd_ref[h]).astype(BF16)
        s_scr[h] = s * sd_ref[h] + _dot_tn(kd, v)
        o_ref[0, :, h * DV_RET:(h + 1) * DV_RET] = _rms(o).astype(o_ref.dtype)

    @pl.when(c == pl.num_programs(2) - 1)
    def _():
        s_out_ref[0] = s_scr[...]


def _retention_tables(n_heads, c):
    lg = jnp.log1p(-jnp.exp2(-5.0 - jnp.arange(n_heads, dtype=F32)))
    i = jnp.arange(c, dtype=F32)
    diff = i[:, None] - i[None, :]
    decay = jnp.where(diff >= 0, jnp.exp(lg[:, None, None] * jnp.maximum(diff, 0.0)), 0.0)
    q_decay = jnp.exp(lg[:, None] * (i + 1.0))
    k_decay = jnp.exp(lg[:, None] * (c - 1.0 - i))
    s_decay = jnp.exp(lg * c)
    return decay, q_decay[:, :, None], k_decay[:, :, None], s_decay[:, None, None]


def _retention(qk, vg, s_in, chunk, hb):
    b, l, _ = qk.shape
    n_heads = s_in.shape[1]
    nhb = n_heads // hb
    per_batch_state = s_in.shape[0] != 1
    tables = _retention_tables(n_heads, chunk)
    return pl.pallas_call(
        functools.partial(_retention_body, hb=hb),
        grid=(b, nhb, l // chunk),
        in_specs=[pl.BlockSpec((1, chunk, hb * DK_RET), lambda bi, hi, ci: (bi, ci, hi)),
                  pl.BlockSpec((1, chunk, hb * DK_RET), lambda bi, hi, ci: (bi, ci, nhb + hi)),
                  pl.BlockSpec((1, chunk, hb * DV_RET), lambda bi, hi, ci: (bi, ci, hi)),
                  pl.BlockSpec((1, hb, DK_RET, DV_RET),
                               (lambda bi, hi, ci: (bi, hi, 0, 0)) if per_batch_state
                               else (lambda bi, hi, ci: (0, hi, 0, 0))),
                  pl.BlockSpec((hb, chunk, chunk), lambda bi, hi, ci: (hi, 0, 0)),
                  pl.BlockSpec((hb, chunk, 1), lambda bi, hi, ci: (hi, 0, 0)),
                  pl.BlockSpec((hb, chunk, 1), lambda bi, hi, ci: (hi, 0, 0)),
                  pl.BlockSpec((hb, 1, 1), lambda bi, hi, ci: (hi, 0, 0))],
        out_specs=[pl.BlockSpec((1, chunk, hb * DV_RET), lambda bi, hi, ci: (bi, ci, hi)),
                   pl.BlockSpec((1, hb, DK_RET, DV_RET), lambda bi, hi, ci: (bi, hi, 0, 0))],
        out_shape=[jax.ShapeDtypeStruct((b, l, n_heads * DV_RET), BF16),
                   jax.ShapeDtypeStruct((b, n_heads, DK_RET, DV_RET), F32)],
        scratch_shapes=[pltpu.VMEM((hb, DK_RET, DV_RET), F32)],
        compiler_params=_cparams(("parallel", "parallel", "arbitrary")),
        name="retention",
    )(qk, qk, vg, s_in, *tables)


def _float_key(x):
    b = lax.bitcast_convert_type(x + 0.0, I32)
    return jnp.where(b < 0, b ^ jnp.int32(0x7FFFFFFF), b)


def _count_ge(key_ref, t):
    r, w = key_ref.shape
    acc = jnp.zeros((r, LANES), F32)
    for c in range(w // LANES):
        acc = acc + jnp.where(key_ref[:, c * LANES:(c + 1) * LANES] >= t, 1.0, 0.0)
    return jnp.sum(acc, axis=-1, keepdims=True)


def _kth_largest_key(key_ref, k):
    r = key_ref.shape[0]
    zero = jnp.zeros((r, 1), I32)
    base = jnp.where(_count_ge(key_ref, zero) >= k, zero, jnp.full((r, 1), -2 ** 31, I32))

    def body(i, base):
        cand = base | jnp.left_shift(jnp.int32(1), 30 - i)
        return jnp.where(_count_ge(key_ref, cand) >= k, cand, base)

    return lax.fori_loop(0, 31, body, base)


def _selection_mask(key_ref, mask_ref, k):
    thr = _kth_largest_key(key_ref, k)
    thr = jnp.maximum(thr, KEY_NEG_INF + 1)
    w = key_ref.shape[1]
    for c in range(w // LANES):
        sl = slice(c * LANES, (c + 1) * LANES)
        mask_ref[:, sl] = jnp.where(key_ref[:, sl] >= thr, 0.0, -jnp.inf).astype(mask_ref.dtype)


def _indexer_keys(iq_stack, iw_cols, ik, valid):
    r = valid.shape[0]
    s = _dot_nt(iq_stack, ik)
    acc = None
    for h in range(N_IDX_HEADS):
        t = jnp.maximum(s[h * r:(h + 1) * r], 0.0) * iw_cols[h]
        acc = t if acc is None else acc + t
    return jnp.where(valid, _float_key(acc), KEY_NEG_INF)


def _dsa_prompt_body(aq_ref, iq_ref, iw_ref, k_ref, v_ref, ik_ref, mk_ref, mv_ref, mik_ref,
                     o_ref, key_scr, mask_scr, *, topk, key_chunk):
    r = aq_ref.shape[1]
    w = k_ref.shape[1]
    wm = mk_ref.shape[0]
    n_heads = aq_ref.shape[2] // D_HEAD
    group = n_heads // N_KV_HEADS
    qpos = N_META + pl.program_id(1) * r + lax.broadcasted_iota(I32, (r, 1), 0)

    iq_stack = iq_ref[...].reshape(N_IDX_HEADS * r, D_IDX)
    iw = iw_ref[0]
    iw_cols = [iw[:, h:h + 1] for h in range(N_IDX_HEADS)]
    for c0 in range(0, w, key_chunk):
        pos = N_META + c0 + lax.broadcasted_iota(I32, (1, key_chunk), 1)
        key_scr[:, c0:c0 + key_chunk] = _indexer_keys(iq_stack, iw_cols, ik_ref[0, c0:c0 + key_chunk, :],
                                                      pos <= qpos)
    mcol = lax.broadcasted_iota(I32, (1, wm), 1)
    key_scr[:, w:w + wm] = _indexer_keys(iq_stack, iw_cols, mik_ref[...],
                                         jnp.broadcast_to(mcol < N_META, (r, wm)))

    _selection_mask(key_scr, mask_scr, topk)

    mask_real = mask_scr[:, :w]
    mask_meta = mask_scr[:, w:w + wm]
    for n in range(N_KV_HEADS):
        qs = jnp.concatenate([aq_ref[0, :, (n * group + g) * D_HEAD:(n * group + g + 1) * D_HEAD]
                              for g in range(group)], axis=0)
        kn = k_ref[0, :, n * D_HEAD:(n + 1) * D_HEAD]
        vn = v_ref[0, :, n * D_HEAD:(n + 1) * D_HEAD]
        s_real = _dot_nt(qs, kn).reshape(group, r, w) + mask_real[None]
        s_meta = _dot_nt(qs, mk_ref[:, n * D_HEAD:(n + 1) * D_HEAD]).reshape(group, r, wm) + mask_meta[None]
        m = jnp.maximum(jnp.max(s_real, axis=-1, keepdims=True), jnp.max(s_meta, axis=-1, keepdims=True))
        p_real = jnp.exp(s_real - m)
        p_meta = jnp.exp(s_meta - m)
        denom = jnp.sum(p_real, axis=-1, keepdims=True) + jnp.sum(p_meta, axis=-1, keepdims=True)
        o = (_dot(p_real.reshape(group * r, w).astype(BF16), vn)
             + _dot(p_meta.reshape(group * r, wm).astype(BF16), mv_ref[:, n * D_HEAD:(n + 1) * D_HEAD]))
        o = o / denom.reshape(group * r, 1)
        for g in range(group):
            o_ref[0, :, (n * group + g) * D_HEAD:(n * group + g + 1) * D_HEAD] = (
                o[g * r:(g + 1) * r].astype(o_ref.dtype))


def _dsa_prompt(aq, iq_hm, iw, kb, vb, ikb, mk, mv, mik, topk):
    b, s, dq = aq.shape
    r = Q_BLOCK
    nqb = s // r
    wm = mk.shape[0]
    return pl.pallas_call(
        functools.partial(_dsa_prompt_body, topk=topk, key_chunk=512),
        grid=(b, nqb),
        in_specs=[pl.BlockSpec((1, r, dq), lambda bi, j: (bi, j, 0)),
                  pl.BlockSpec((N_IDX_HEADS, r, D_IDX), lambda bi, j: (0, bi * nqb + j, 0)),
                  pl.BlockSpec((1, r, N_IDX_HEADS), lambda bi, j: (bi, j, 0)),
                  pl.BlockSpec((1, s, _KV_W), lambda bi, j: (bi, 0, 0)),
                  pl.BlockSpec((1, s, _KV_W), lambda bi, j: (bi, 0, 0)),
                  pl.BlockSpec((1, s, D_IDX), lambda bi, j: (bi, 0, 0)),
                  pl.BlockSpec(mk.shape, lambda bi, j: (0, 0)),
                  pl.BlockSpec(mv.shape, lambda bi, j: (0, 0)),
                  pl.BlockSpec(mik.shape, lambda bi, j: (0, 0))],
        out_specs=pl.BlockSpec((1, r, dq), lambda bi, j: (bi, j, 0)),
        out_shape=jax.ShapeDtypeStruct((b, s, dq), BF16),
        scratch_shapes=[pltpu.VMEM((r, s + wm), I32), pltpu.VMEM((r, s + wm), F32)],
        compiler_params=_cparams(("parallel", "arbitrary")),
        name="dsa_prompt",
    )(aq, iq_hm, iw, kb, vb, ikb, mk, mv, mik)


def _page_specs(n_pages, width):
    return [pl.BlockSpec((1, PAGE_SIZE, width), functools.partial(lambda bi, pt, p: (pt[bi, p], 0, 0), p=p))
            for p in range(n_pages)]


def _sample_keys_body(pt_ref, iq_ref, iw_ref, ikn_ref, *rest, n_pages):
    pages = rest[:n_pages]
    key_ref = rest[n_pages]
    nq = key_ref.shape[1]
    iq_stack = iq_ref[0].reshape(N_IDX_HEADS * nq, D_IDX).astype(BF16)
    iw = iw_ref[0]
    iw_cols = [iw[h * nq:(h + 1) * nq] for h in range(N_IDX_HEADS)]
    all_valid = jnp.full((nq, PAGE_SIZE), True)
    for p in range(n_pages):
        key_ref[0, :, p * PAGE_SIZE:(p + 1) * PAGE_SIZE] = _indexer_keys(
            iq_stack, iw_cols, pages[p][0].astype(BF16), all_valid)
    wn = ikn_ref.shape[1]
    col = lax.broadcasted_iota(I32, (nq, wn), 1)
    row = lax.broadcasted_iota(I32, (nq, wn), 0)
    key_ref[0, :, n_pages * PAGE_SIZE:n_pages * PAGE_SIZE + wn] = _indexer_keys(
        iq_stack, iw_cols, ikn_ref[0], col <= row)


def _sample_keys(page_table, iq4, iw_col, ik_new, cache_kidx):
    db, _, nq, _ = iq4.shape
    n_pages = page_table.shape[1]
    wn = ik_new.shape[1]
    wt = n_pages * PAGE_SIZE + wn
    grid_spec = pltpu.PrefetchScalarGridSpec(
        num_scalar_prefetch=1,
        grid=(db,),
        in_specs=[pl.BlockSpec((1, N_IDX_HEADS, nq, D_IDX), lambda bi, pt: (bi, 0, 0, 0)),
                  pl.BlockSpec((1, N_IDX_HEADS * nq, 1), lambda bi, pt: (bi, 0, 0)),
                  pl.BlockSpec((1, wn, D_IDX), lambda bi, pt: (bi, 0, 0))] + _page_specs(n_pages, D_IDX),
        out_specs=pl.BlockSpec((1, nq, wt), lambda bi, pt: (bi, 0, 0)),
    )
    return pl.pallas_call(
        functools.partial(_sample_keys_body, n_pages=n_pages),
        grid_spec=grid_spec,
        out_shape=jax.ShapeDtypeStruct((db, nq, wt), I32),
        compiler_params=_cparams(("arbitrary",)),
        name="sample_keys",
    )(page_table, iq4, iw_col, ik_new, *([cache_kidx] * n_pages))


def _select_body(key_ref, mask_ref, *, topk):
    _selection_mask(key_ref, mask_ref, topk)


def _select_mask(keys, topk, tr):
    rows, wt = keys.shape
    return pl.pallas_call(
        functools.partial(_select_body, topk=topk),
        grid=(rows // tr,),
        in_specs=[pl.BlockSpec((tr, wt), lambda i: (i, 0))],
        out_specs=pl.BlockSpec((tr, wt), lambda i: (i, 0)),
        out_shape=jax.ShapeDtypeStruct((rows, wt), F32),
        compiler_params=_cparams(("parallel",)),
        name="select_mask",
    )(keys)


def _sample_attn_body(pt_ref, q_ref, mask_ref, kn_ref, vn_ref, *rest, n_pages):
    kpages = rest[:n_pages]
    vpages = rest[n_pages:2 * n_pages]
    o_ref = rest[2 * n_pages]
    n_heads, nq = q_ref.shape[1], q_ref.shape[2]
    group = n_heads // N_KV_HEADS
    q = q_ref[0].reshape(n_heads * nq, D_HEAD).astype(BF16)
    wn = kn_ref.shape[1]
    chunks = [(p * PAGE_SIZE, PAGE_SIZE) for p in range(n_pages)] + [(n_pages * PAGE_SIZE, wn)]
    for n in range(N_KV_HEADS):
        qn = q[n * group * nq:(n + 1) * group * nq]
        hs = slice(n * D_HEAD, (n + 1) * D_HEAD)
        ks = [kpages[p][0, :, hs].astype(BF16) for p in range(n_pages)] + [kn_ref[0, :, hs]]
        vs = [vpages[p][0, :, hs].astype(BF16) for p in range(n_pages)] + [vn_ref[0, :, hs]]
        scores = []
        for (c0, cw), kc in zip(chunks, ks):
            mk = mask_ref[0, :, c0:c0 + cw]
            scores.append((_dot_nt(qn, kc).reshape(group, nq, cw) + mk[None]).reshape(group * nq, cw))
        m = functools.reduce(jnp.maximum, [jnp.max(sc, axis=-1, keepdims=True) for sc in scores])
        ps = [jnp.exp(sc - m) for sc in scores]
        denom = functools.reduce(jnp.add, [jnp.sum(pc, axis=-1, keepdims=True) for pc in ps])
        o = functools.reduce(jnp.add, [_dot(pc.astype(BF16), vc) for pc, vc in zip(ps, vs)])
        o_ref[0, n * group:(n + 1) * group] = (o / denom).reshape(group, nq, D_HEAD)


def _sample_attn(page_table, q4, mask, k_new, v_new, cache_k, cache_v):
    db, n_heads, nq, _ = q4.shape
    n_pages = page_table.shape[1]
    wn = k_new.shape[1]
    wt = mask.shape[2]
    grid_spec = pltpu.PrefetchScalarGridSpec(
        num_scalar_prefetch=1,
        grid=(db,),
        in_specs=[pl.BlockSpec((1, n_heads, nq, D_HEAD), lambda bi, pt: (bi, 0, 0, 0)),
                  pl.BlockSpec((1, nq, wt), lambda bi, pt: (bi, 0, 0)),
                  pl.BlockSpec((1, wn, _KV_W), lambda bi, pt: (bi, 0, 0)),
                  pl.BlockSpec((1, wn, _KV_W), lambda bi, pt: (bi, 0, 0))]
        + _page_specs(n_pages, _KV_W) + _page_specs(n_pages, _KV_W),
        out_specs=pl.BlockSpec((1, n_heads, nq, D_HEAD), lambda bi, pt: (bi, 0, 0, 0)),
    )
    return pl.pallas_call(
        functools.partial(_sample_attn_body, n_pages=n_pages),
        grid_spec=grid_spec,
        out_shape=jax.ShapeDtypeStruct((db, n_heads, nq, D_HEAD), F32),
        compiler_params=_cparams(("arbitrary",)),
        name="sample_attn",
    )(page_table, q4, mask, k_new, v_new, *([cache_k] * n_pages), *([cache_v] * n_pages))


def _merge_body(oret_ref, rg_ref, odsa_ref, gates_ref, h_ref, wr_ref, wd_ref, wo_ref, nw_ref,
                h1_ref, hn_ref):
    d = h_ref.shape[1]
    rg = rg_ref[...].astype(F32)
    a = oret_ref[...].astype(F32) * (rg / (1.0 + jnp.exp(-rg)))
    br_ret = _dot(a.astype(BF16), wr_ref[...])
    br_dsa = _dot(odsa_ref[...], wd_ref[...])
    mix = gates_ref[:, :d].astype(F32) * br_ret + gates_ref[:, d:].astype(F32) * br_dsa
    h1 = h_ref[...] + _dot(mix.astype(BF16), wo_ref[...])
    h1_ref[...] = h1
    hn_ref[...] = (_rms(h1) * nw_ref[...]).astype(hn_ref.dtype)


def _merge(oret, vg, odsa, gates, h, w_ret, w_dsa, w_out, norm_w, tm):
    t, d = h.shape
    rw = oret.shape[1]
    resident = lambda a: pl.BlockSpec(a.shape, lambda i: (0, 0))
    return pl.pallas_call(
        _merge_body,
        grid=(t // tm,),
        in_specs=[pl.BlockSpec((tm, rw), lambda i: (i, 0)),
                  pl.BlockSpec((tm, rw), lambda i: (i, 1)),
                  pl.BlockSpec((tm, d), lambda i: (i, 0)),
                  pl.BlockSpec((tm, 2 * d), lambda i: (i, 0)),
                  pl.BlockSpec((tm, d), lambda i: (i, 0)),
                  resident(w_ret), resident(w_dsa), resident(w_out), resident(norm_w)],
        out_specs=[pl.BlockSpec((tm, d), lambda i: (i, 0)), pl.BlockSpec((tm, d), lambda i: (i, 0))],
        out_shape=[jax.ShapeDtypeStruct((t, d), F32), jax.ShapeDtypeStruct((t, d), BF16)],
        compiler_params=_cparams(("parallel",)),
        name="merge",
    )(oret, vg, odsa, gates, h, w_ret, w_dsa, w_out, norm_w)


def _ffn_body(hn_ref, h1_ref, wa_ref, wb_ref, wo_ref, y_ref, acc_ref):
    f = pl.program_id(1)
    a = _dot(hn_ref[...], wa_ref[...])
    b = _dot(hn_ref[...], wb_ref[...])
    g = (a / (1.0 + jnp.exp(-a)) * b).astype(BF16)
    part = _dot(g, wo_ref[...])

    @pl.when(f == 0)
    def _():
        acc_ref[...] = h1_ref[...] + part

    @pl.when(f != 0)
    def _():
        acc_ref[...] = acc_ref[...] + part

    @pl.when(f == pl.num_programs(1) - 1)
    def _():
        y_ref[...] = acc_ref[...]


def _ffn(hn, h1, w_in, w_out, tm, tf):
    t, d = h1.shape
    dff = w_out.shape[0]
    nf = dff // tf
    return pl.pallas_call(
        _ffn_body,
        grid=(t // tm, nf),
        in_specs=[pl.BlockSpec((tm, d), lambda i, f: (i, 0)),
                  pl.BlockSpec((tm, d), lambda i, f: (i, 0)),
                  pl.BlockSpec((d, tf), lambda i, f: (0, f)),
                  pl.BlockSpec((d, tf), lambda i, f: (0, nf + f)),
                  pl.BlockSpec((tf, d), lambda i, f: (f, 0))],
        out_specs=pl.BlockSpec((tm, d), lambda i, f: (i, 0)),
        out_shape=jax.ShapeDtypeStruct((t, d), F32),
        scratch_shapes=[pltpu.VMEM((tm, d), F32)],
        compiler_params=_cparams(("parallel", "arbitrary")),
        name="ffn",
    )(hn, h1, w_in, w_in, w_out)


def _rope_tables(pos):
    half = DK_RET // 2
    inv = ROPE_BASE ** (-jnp.arange(half, dtype=F32) / half)
    ang = pos.astype(F32)[:, None] * inv[None, :]
    return jnp.cos(ang), jnp.sin(ang)


def _row_tile(t, cap):
    tm = min(t, cap)
    assert t % tm == 0
    return tm


def _pad_rows(a, rows):
    return jnp.pad(a, [(0, rows - a.shape[0])] + [(0, 0)] * (a.ndim - 1))


def kernel(x_prompt, x_sample, cache_k, cache_v, cache_kidx, state_ret, page_table, meta_tokens,
           norm_mix_w, w_in, w_ret_proj, dsa_q_norm_w, dsa_k_norm_w, idx_k_norm_w, idx_k_norm_b,
           w_dsa_proj, w_out, norm_ffn_w, w_ffn_in, w_ffn_out):
    assert w_in.shape[0] == 1, "single-layer trunk"
    b, seq, d = x_prompt.shape
    db, dseq, _ = x_sample.shape
    n_meta = meta_tokens.shape[0]
    assert n_meta == N_META
    h_ret = d // DK_RET
    n_heads = d // D_HEAD
    rqk_w, rv_w = h_ret * DK_RET, h_ret * DV_RET
    n_pages = page_table.shape[1]
    past = n_pages * PAGE_SIZE
    n_pool = cache_k.shape[1]

    wi = w_in[0]
    c0 = 0
    w_rqk = wi[:, c0:c0 + 2 * rqk_w].astype(BF16); c0 += 2 * rqk_w
    w_rvg = wi[:, c0:c0 + 2 * rv_w].astype(BF16); c0 += 2 * rv_w
    w_aq = wi[:, c0:c0 + n_heads * D_HEAD].astype(BF16); c0 += n_heads * D_HEAD
    w_ak = wi[:, c0:c0 + _KV_W]; c0 += _KV_W
    w_av = wi[:, c0:c0 + _KV_W]; c0 += _KV_W
    w_iq = wi[:, c0:c0 + _IQ_W]; c0 += _IQ_W
    w_ik = wi[:, c0:c0 + D_IDX]; c0 += D_IDX
    w_iw = wi[:, c0:c0 + N_IDX_HEADS]; c0 += N_IDX_HEADS
    w_gz = wi[:, c0:c0 + 2 * d].astype(BF16); c0 += 2 * d
    assert c0 == wi.shape[1]
    zpad = lambda n: jnp.zeros((d, n), F32)
    w_small = jnp.concatenate([w_ak, w_av, w_iq, w_ik, zpad(LANES - D_IDX), w_iw, zpad(LANES - N_IDX_HEADS)],
                              axis=1).astype(BF16)
    assert w_small.shape[1] == _S_END
    q_gain = dsa_q_norm_w[0].reshape(1, D_HEAD)
    k_gain = dsa_k_norm_w[0].reshape(1, D_HEAD)
    ln_w = idx_k_norm_w[0].reshape(1, D_IDX)
    ln_b = idx_k_norm_b[0].reshape(1, D_IDX)
    w_ret_b = w_ret_proj[0].astype(BF16)
    w_dsa_b = w_dsa_proj[0].astype(BF16)
    w_out_b = w_out[0].astype(BF16)
    w_ffn_in_b = w_ffn_in[0].astype(BF16)
    w_ffn_out_b = w_ffn_out[0].astype(BF16)
    ffn_gain = norm_ffn_w[0].reshape(1, d)

    def project(x2d, pos, tm, want_queries):
        xn = _rmsnorm_bf16(x2d, norm_mix_w[0], tm)
        cos, sin = _rope_tables(pos)
        reps = max(1, tm // pos.shape[0])
        cos, sin = jnp.tile(cos, (reps, 1)), jnp.tile(sin, (reps, 1))
        out = {}
        out["rqk"] = _proj_rotary(xn, w_rqk, cos, sin, tm, rqk_w)
        out["rvg"] = _proj_simple(_proj_plain_body, xn, w_rvg, tm, 1024, "proj_rvg")
        (out["k"], out["v"], out["kb"], out["vb"], out["iq"], out["ik"], out["ikb"], out["iw"]) = _proj_small(
            xn, w_small, k_gain, ln_w, ln_b, tm)
        if want_queries:
            out["aq"] = _proj_simple(_proj_headnorm_body, xn, w_aq, tm, n_heads * D_HEAD, "proj_aq", (q_gain,))
            out["gates"] = _proj_simple(_proj_sigmoid_body, xn, w_gz, tm, 1024, "proj_gates")
        return out

    def post(x2d, oret, rvg, odsa, gates, tm):
        h1, hn = _merge(oret, rvg, odsa, gates, x2d, w_ret_b, w_dsa_b, w_out_b, ffn_gain, tm)
        dff = w_ffn_out_b.shape[0]
        tf = dff // 2 if (dff // 2) % LANES == 0 else dff
        return _ffn(hn, h1, w_ffn_in_b, w_ffn_out_b, tm, tf)

    pm = project(meta_tokens, jnp.arange(n_meta), n_meta, False)
    s_zero = jnp.zeros((1, h_ret, DK_RET, DV_RET), F32)
    _, s_meta = _retention(pm["rqk"][None], pm["rvg"][None], s_zero, n_meta, 1)

    tp = b * seq
    xp = x_prompt.reshape(tp, d)
    tm_p = _row_tile(seq, 512)
    pp = project(xp, n_meta + jnp.arange(seq), tm_p, True)
    oret_p, s_prompt = _retention(pp["rqk"].reshape(b, seq, -1), pp["rvg"].reshape(b, seq, -1),
                                  s_meta, RET_CHUNK, 1)
    topk_p = min(INDEX_TOPK, seq // 4)
    odsa_p = _dsa_prompt(pp["aq"].reshape(b, seq, -1), pp["iq"], pp["iw"].reshape(b, seq, -1),
                         pp["kb"].reshape(b, seq, -1), pp["vb"].reshape(b, seq, -1),
                         pp["ikb"].reshape(b, seq, -1),
                         _pad_rows(pm["kb"], LANES), _pad_rows(pm["vb"], LANES), _pad_rows(pm["ikb"], LANES),
                         topk_p)
    y_prompt = post(xp, oret_p.reshape(tp, -1), pp["rvg"], odsa_p.reshape(tp, -1), pp["gates"], tm_p)

    ts = db * dseq
    xs = x_sample.reshape(ts, d)
    tm_s = _row_tile(ts, 512)
    ps = project(xs, past + jnp.arange(dseq), tm_s, True)
    oret_s, s_sample = _retention(ps["rqk"].reshape(db, dseq, -1), ps["rvg"].reshape(db, dseq, -1),
                                  state_ret[0], dseq, h_ret)
    topk_s = min(INDEX_TOPK, (past + dseq) // 4)
    iq4 = ps["iq"].reshape(N_IDX_HEADS, db, dseq, D_IDX).transpose(1, 0, 2, 3).astype(F32)
    iw_col = ps["iw"].reshape(db, dseq, N_IDX_HEADS).transpose(0, 2, 1).reshape(db, N_IDX_HEADS * dseq, 1)
    pad_new = lambda a: jnp.pad(a.reshape(db, dseq, -1), ((0, 0), (0, LANES - dseq), (0, 0)))
    keys_s = _sample_keys(page_table, iq4, iw_col, pad_new(ps["ikb"]), cache_kidx[0])
    wt = keys_s.shape[2]
    mask_s = _select_mask(keys_s.reshape(ts, wt), topk_s, _row_tile(ts, 256)).reshape(db, dseq, wt)
    q4 = ps["aq"].reshape(db, dseq, n_heads, D_HEAD).transpose(0, 2, 1, 3).astype(F32)
    o4 = _sample_attn(page_table, q4, mask_s, pad_new(ps["kb"]), pad_new(ps["vb"]),
                      cache_k[0].reshape(n_pool, PAGE_SIZE, _KV_W), cache_v[0].reshape(n_pool, PAGE_SIZE, _KV_W))
    odsa_s = o4.transpose(0, 2, 1, 3).reshape(ts, n_heads * D_HEAD).astype(BF16)
    y_sample = post(xs, oret_s.reshape(ts, -1), ps["rvg"], odsa_s, ps["gates"], tm_s)

    def with_meta(meta, real, tail):
        m = jnp.broadcast_to(meta[None], (b,) + meta.shape)
        return jnp.concatenate([m, real.reshape(b, seq, -1)], axis=1).reshape((1, b, n_meta + seq) + tail)

    kv_tail = (N_KV_HEADS, D_HEAD)
    return (y_prompt.reshape(b, seq, d),
            y_sample.reshape(db, dseq, d),
            with_meta(pm["k"], pp["k"], kv_tail),
            with_meta(pm["v"], pp["v"], kv_tail),
            with_meta(pm["ik"], pp["ik"], (D_IDX,)),
            s_prompt[None],
            ps["k"].reshape((1, db, dseq) + kv_tail),
            ps["v"].reshape((1, db, dseq) + kv_tail),
            ps["ik"].reshape(1, db, dseq, D_IDX),
            s_sample[None])
```

```python
import functools

import numpy as np
import jax
import jax.numpy as jnp
from jax import lax
from jax.experimental import pallas as pl
from jax.experimental.pallas import tpu as pltpu

F32 = jnp.float32
BF16 = jnp.bfloat16
I32 = jnp.int32

N_META = 16
DK_RET = 256
DV_RET = 512
RET_CHUNK = 128
ROPE_BASE = 10000.0
D_HEAD = 128
N_KV_HEADS = 2
N_IDX_HEADS = 8
D_IDX = 64
INDEX_TOPK = 256
Q_BLOCK = 128
PAGE_SIZE = 128
NORM_EPS = 1e-6
IDX_SCALE = (N_IDX_HEADS ** -0.5) * (D_IDX ** -0.5)

LANES = 128
VMEM_LIMIT = 56 * 1024 * 1024

KEY_NEG_INF = int(np.array(0xFF800000 ^ 0x7FFFFFFF, np.uint32).view(np.int32))
POS_PAD = 1 << 30


def _cparams(sem):
    return pltpu.CompilerParams(dimension_semantics=sem, vmem_limit_bytes=VMEM_LIMIT)


def _dot(a, b):
    return jnp.dot(a, b, preferred_element_type=F32)


def _dot_nt(a, b):
    return lax.dot_general(a, b, (((1,), (1,)), ((), ())), preferred_element_type=F32)


def _dot_tn(a, b):
    return lax.dot_general(a, b, (((0,), (0,)), ((), ())), preferred_element_type=F32)


def _rms(x):
    return x * lax.rsqrt(jnp.mean(x * x, axis=-1, keepdims=True) + NORM_EPS)


def _norm_body(x_ref, w_ref, o_ref):
    o_ref[...] = (_rms(x_ref[...]) * w_ref[...]).astype(o_ref.dtype)


def _rmsnorm_bf16(x, w, tm):
    t, d = x.shape
    return pl.pallas_call(
        _norm_body,
        grid=(t // tm,),
        in_specs=[pl.BlockSpec((tm, d), lambda i: (i, 0)), pl.BlockSpec((1, d), lambda i: (0, 0))],
        out_specs=pl.BlockSpec((tm, d), lambda i: (i, 0)),
        out_shape=jax.ShapeDtypeStruct((t, d), BF16),
        compiler_params=_cparams(("parallel",)),
        name="rmsnorm",
    )(x, w.reshape(1, d))


def _proj_rotary_body(x_ref, w_ref, cos_ref, sin_ref, o_ref, *, q_blocks):
    z = _dot(x_ref[...], w_ref[...])
    scale = jnp.where(pl.program_id(0) >= q_blocks, DK_RET ** -0.5, 1.0).astype(F32)
    cos = cos_ref[...]
    sin = sin_ref[...]
    half = DK_RET // 2
    for h in range(z.shape[1] // DK_RET):
        x1 = z[:, h * DK_RET:h * DK_RET + half]
        x2 = z[:, h * DK_RET + half:(h + 1) * DK_RET]
        o_ref[:, h * DK_RET:h * DK_RET + half] = ((x1 * cos - x2 * sin) * scale).astype(o_ref.dtype)
        o_ref[:, h * DK_RET + half:(h + 1) * DK_RET] = ((x1 * sin + x2 * cos) * scale).astype(o_ref.dtype)


def _proj_rotary(xn, w, cos, sin, tm, tn):
    t, d = xn.shape
    n = w.shape[1]
    pos_blocks = cos.shape[0] // tm
    return pl.pallas_call(
        functools.partial(_proj_rotary_body, q_blocks=(n // 2) // tn),
        grid=(n // tn, t // tm),
        in_specs=[pl.BlockSpec((tm, d), lambda j, i: (i, 0)),
                  pl.BlockSpec((d, tn), lambda j, i: (0, j)),
                  pl.BlockSpec((tm, DK_RET // 2), lambda j, i: (i % pos_blocks, 0)),
                  pl.BlockSpec((tm, DK_RET // 2), lambda j, i: (i % pos_blocks, 0))],
        out_specs=pl.BlockSpec((tm, tn), lambda j, i: (i, j)),
        out_shape=jax.ShapeDtypeStruct((t, n), BF16),
        compiler_params=_cparams(("parallel", "parallel")),
        name="proj_rotary",
    )(xn, w, cos, sin)


def _proj_plain_body(x_ref, w_ref, o_ref):
    o_ref[...] = _dot(x_ref[...], w_ref[...]).astype(o_ref.dtype)


def _proj_sigmoid_body(x_ref, w_ref, o_ref):
    z = _dot(x_ref[...], w_ref[...])
    o_ref[...] = (1.0 / (1.0 + jnp.exp(-z))).astype(o_ref.dtype)


def _proj_headnorm_body(x_ref, w_ref, g_ref, o_ref):
    z = _dot(x_ref[...], w_ref[...])
    g = g_ref[...] * (D_HEAD ** -0.5)
    for h in range(z.shape[1] // D_HEAD):
        o_ref[:, h * D_HEAD:(h + 1) * D_HEAD] = (_rms(z[:, h * D_HEAD:(h + 1) * D_HEAD]) * g).astype(o_ref.dtype)


def _proj_simple(body, xn, w, tm, tn, name, extra=()):
    t, d = xn.shape
    n = w.shape[1]
    extra_specs = [pl.BlockSpec(e.shape, lambda j, i: (0, 0)) for e in extra]
    return pl.pallas_call(
        body,
        grid=(n // tn, t // tm),
        in_specs=[pl.BlockSpec((tm, d), lambda j, i: (i, 0)),
                  pl.BlockSpec((d, tn), lambda j, i: (0, j))] + extra_specs,
        out_specs=pl.BlockSpec((tm, tn), lambda j, i: (i, j)),
        out_shape=jax.ShapeDtypeStruct((t, n), BF16),
        compiler_params=_cparams(("parallel", "parallel")),
        name=name,
    )(xn, w, *extra)


_KV_W = N_KV_HEADS * D_HEAD
_IQ_W = N_IDX_HEADS * D_IDX
_S_K = 0
_S_V = _KV_W
_S_IQ = 2 * _KV_W
_S_IK = _S_IQ + _IQ_W
_S_IW = _S_IK + LANES
_S_END = _S_IW + LANES


def _proj_small_body(x_ref, w_ref, kg_ref, lnw_ref, lnb_ref,
                     k_ref, v_ref, kb_ref, vb_ref, iq_ref, ik_ref, ikb_ref, iw_ref):
    z = _dot(x_ref[...], w_ref[...])
    kg = kg_ref[...]
    for h in range(N_KV_HEADS):
        kh = _rms(z[:, _S_K + h * D_HEAD:_S_K + (h + 1) * D_HEAD]) * kg
        k_ref[:, h * D_HEAD:(h + 1) * D_HEAD] = kh
        kb_ref[:, h * D_HEAD:(h + 1) * D_HEAD] = kh.astype(BF16)
    v = z[:, _S_V:_S_V + _KV_W]
    v_ref[...] = v
    vb_ref[...] = v.astype(BF16)
    for h in range(N_IDX_HEADS):
        iq_ref[h] = z[:, _S_IQ + h * D_IDX:_S_IQ + (h + 1) * D_IDX].astype(BF16)
    ik = z[:, _S_IK:_S_IK + D_IDX]
    mu = jnp.mean(ik, axis=-1, keepdims=True)
    var = jnp.mean(jnp.square(ik - mu), axis=-1, keepdims=True)
    ikn = (ik - mu) * lax.rsqrt(var + NORM_EPS) * lnw_ref[...] + lnb_ref[...]
    ik_ref[...] = ikn
    ikb_ref[...] = ikn.astype(BF16)
    iw_ref[...] = z[:, _S_IW:_S_IW + N_IDX_HEADS] * IDX_SCALE


def _proj_small(xn, w, k_gain, ln_w, ln_b, tm):
    t, d = xn.shape
    row = lambda width: pl.BlockSpec((tm, width), lambda i: (i, 0))
    full = lambda a: pl.BlockSpec(a.shape, lambda i: (0, 0))
    return pl.pallas_call(
        _proj_small_body,
        grid=(t // tm,),
        in_specs=[row(d), full(w), full(k_gain), full(ln_w), full(ln_b)],
        out_specs=[row(_KV_W), row(_KV_W), row(_KV_W), row(_KV_W),
                   pl.BlockSpec((N_IDX_HEADS, tm, D_IDX), lambda i: (0, i, 0)),
                   row(D_IDX), row(D_IDX), row(N_IDX_HEADS)],
        out_shape=[jax.ShapeDtypeStruct((t, _KV_W), F32), jax.ShapeDtypeStruct((t, _KV_W), F32),
                   jax.ShapeDtypeStruct((t, _KV_W), BF16), jax.ShapeDtypeStruct((t, _KV_W), BF16),
                   jax.ShapeDtypeStruct((N_IDX_HEADS, t, D_IDX), BF16),
                   jax.ShapeDtypeStruct((t, D_IDX), F32), jax.ShapeDtypeStruct((t, D_IDX), BF16),
                   jax.ShapeDtypeStruct((t, N_IDX_HEADS), F32)],
        compiler_params=_cparams(("parallel",)),
        name="proj_small",
    )(xn, w, k_gain, ln_w, ln_b)


def _retention_body(q_ref, k_ref, v_ref, s_in_ref, dec_ref, qd_ref, kd_ref, sd_ref,
                    o_ref, s_out_ref, s_scr, *, hb):
    c = pl.program_id(2)

    @pl.when(c == 0)
    def _():
        s_scr[...] = s_in_ref[0]

    for h in range(hb):
        q = q_ref[0, :, h * DK_RET:(h + 1) * DK_RET]
        k = k_ref[0, :, h * DK_RET:(h + 1) * DK_RET]
        v = v_ref[0, :, h * DV_RET:(h + 1) * DV_RET]
        s = s_scr[h]
        scores = _dot_nt(q, k) * dec_ref[h]
        o = _dot(scores.astype(BF16), v) + _dot(q, s.astype(BF16)) * qd_ref[h]
        kd = (k.astype(F32) * kd_ref[h]).astype(BF16)
        s_scr[h] = s * sd_ref[h] + _dot_tn(kd, v)
        o_ref[0, :, h * DV_RET:(h + 1) * DV_RET] = _rms(o).astype(o_ref.dtype)

    @pl.when(c == pl.num_programs(2) - 1)
    def _():
        s_out_ref[0] = s_scr[...]


def _retention_tables(n_heads, c):
    lg = jnp.log1p(-jnp.exp2(-5.0 - jnp.arange(n_heads, dtype=F32)))
    i = jnp.arange(c, dtype=F32)
    diff = i[:, None] - i[None, :]
    decay = jnp.where(diff >= 0, jnp.exp(lg[:, None, None] * jnp.maximum(diff, 0.0)), 0.0)
    q_decay = jnp.exp(lg[:, None] * (i + 1.0))
    k_decay = jnp.exp(lg[:, None] * (c - 1.0 - i))
    s_decay = jnp.exp(lg * c)
    return decay, q_decay[:, :, None], k_decay[:, :, None], s_decay[:, None, None]


def _retention(qk, vg, s_in, chunk, hb):
    b, l, _ = qk.shape
    n_heads = s_in.shape[1]
    nhb = n_heads // hb
    per_batch_state = s_in.shape[0] != 1
    tables = _retention_tables(n_heads, chunk)
    return pl.pallas_call(
        functools.partial(_retention_body, hb=hb),
        grid=(b, nhb, l // chunk),
        in_specs=[pl.BlockSpec((1, chunk, hb * DK_RET), lambda bi, hi, ci: (bi, ci, hi)),
                  pl.BlockSpec((1, chunk, hb * DK_RET), lambda bi, hi, ci: (bi, ci, nhb + hi)),
                  pl.BlockSpec((1, chunk, hb * DV_RET), lambda bi, hi, ci: (bi, ci, hi)),
                  pl.BlockSpec((1, hb, DK_RET, DV_RET),
                               (lambda bi, hi, ci: (bi, hi, 0, 0)) if per_batch_state
                               else (lambda bi, hi, ci: (0, hi, 0, 0))),
                  pl.BlockSpec((hb, chunk, chunk), lambda bi, hi, ci: (hi, 0, 0)),
                  pl.BlockSpec((hb, chunk, 1), lambda bi, hi, ci: (hi, 0, 0)),
                  pl.BlockSpec((hb, chunk, 1), lambda bi, hi, ci: (hi, 0, 0)),
                  pl.BlockSpec((hb, 1, 1), lambda bi, hi, ci: (hi, 0, 0))],
        out_specs=[pl.BlockSpec((1, chunk, hb * DV_RET), lambda bi, hi, ci: (bi, ci, hi)),
                   pl.BlockSpec((1, hb, DK_RET, DV_RET), lambda bi, hi, ci: (bi, hi, 0, 0))],
        out_shape=[jax.ShapeDtypeStruct((b, l, n_heads * DV_RET), BF16),
                   jax.ShapeDtypeStruct((b, n_heads, DK_RET, DV_RET), F32)],
        scratch_shapes=[pltpu.VMEM((hb, DK_RET, DV_RET), F32)],
        compiler_params=_cparams(("parallel", "parallel", "arbitrary")),
        name="retention",
    )(qk, qk, vg, s_in, *tables)


def _float_key(x):
    b = lax.bitcast_convert_type(x + 0.0, I32)
    return jnp.where(b < 0, b ^ jnp.int32(0x7FFFFFFF), b)


def _count_ge(key_ref, t):
    r, w = key_ref.shape
    acc = jnp.zeros((r, LANES), F32)
    for c in range(w // LANES):
        acc = acc + jnp.where(key_ref[:, c * LANES:(c + 1) * LANES] >= t, 1.0, 0.0)
    return jnp.sum(acc, axis=-1, keepdims=True)


def _kth_largest_key(key_ref, k):
    r = key_ref.shape[0]
    zero = jnp.zeros((r, 1), I32)
    base = jnp.where(_count_ge(key_ref, zero) >= k, zero, jnp.full((r, 1), -2 ** 31, I32))

    def body(i, base):
        cand = base | jnp.left_shift(jnp.int32(1), 30 - i)
        return jnp.where(_count_ge(key_ref, cand) >= k, cand, base)

    return lax.fori_loop(0, 31, body, base)


def _selection_mask(key_ref, mask_ref, k):
    thr = _kth_largest_key(key_ref, k)
    thr = jnp.maximum(thr, KEY_NEG_INF + 1)
    w = key_ref.shape[1]
    for c in range(w // LANES):
        sl = slice(c * LANES, (c + 1) * LANES)
        mask_ref[:, sl] = jnp.where(key_ref[:, sl] >= thr, 0.0, -jnp.inf).astype(mask_ref.dtype)


def _indexer_keys(iq_stack, iw_cols, ik, valid, keys_transposed=False):
    r = valid.shape[0]
    s = _dot(iq_stack, ik) if keys_transposed else _dot_nt(iq_stack, ik)
    acc = None
    for h in range(N_IDX_HEADS):
        t = jnp.maximum(s[h * r:(h + 1) * r], 0.0) * iw_cols[h]
        acc = t if acc is None else acc + t
    return jnp.where(valid, _float_key(acc), KEY_NEG_INF)


def _dsa_prompt_block(aq_ref, iq_ref, iw_ref, k_ref, v_ref, ik_ref, mk_ref, mv_ref, mik_ref,
                      o_ref, key_scr, mask_scr, *, topk, key_chunk, w):
    r = aq_ref.shape[1]
    wm = mk_ref.shape[0]
    n_heads = aq_ref.shape[2] // D_HEAD
    group = n_heads // N_KV_HEADS
    qpos = N_META + pl.program_id(1) * r + lax.broadcasted_iota(I32, (r, 1), 0)

    iq_stack = iq_ref[...].reshape(N_IDX_HEADS * r, D_IDX)
    iw = iw_ref[0]
    iw_cols = [iw[:, h:h + 1] for h in range(N_IDX_HEADS)]
    mcol = lax.broadcasted_iota(I32, (1, wm), 1)
    key_scr[:, :wm] = _indexer_keys(iq_stack, iw_cols, mik_ref[...], jnp.broadcast_to(mcol < N_META, (r, wm)))
    for c0 in range(0, w, key_chunk):
        pos = N_META + c0 + lax.broadcasted_iota(I32, (1, key_chunk), 1)
        key_scr[:, wm + c0:wm + c0 + key_chunk] = _indexer_keys(
            iq_stack, iw_cols, ik_ref[0, c0:c0 + key_chunk, :], pos <= qpos)

    _selection_mask(key_scr.at[:, :wm + w], mask_scr.at[:, :wm + w], topk)

    mask_meta = mask_scr[:, :wm]
    mask_real = mask_scr[:, wm:wm + w]
    for n in range(N_KV_HEADS):
        qs = jnp.concatenate([aq_ref[0, :, (n * group + g) * D_HEAD:(n * group + g + 1) * D_HEAD]
                              for g in range(group)], axis=0)
        kn = k_ref[0, :w, n * D_HEAD:(n + 1) * D_HEAD]
        vn = v_ref[0, :w, n * D_HEAD:(n + 1) * D_HEAD]
        s_real = _dot_nt(qs, kn).reshape(group, r, w) + mask_real[None]
        s_meta = _dot_nt(qs, mk_ref[:, n * D_HEAD:(n + 1) * D_HEAD]).reshape(group, r, wm) + mask_meta[None]
        m = jnp.maximum(jnp.max(s_real, axis=-1, keepdims=True), jnp.max(s_meta, axis=-1, keepdims=True))
        p_real = jnp.exp(s_real - m)
        p_meta = jnp.exp(s_meta - m)
        denom = jnp.sum(p_real, axis=-1, keepdims=True) + jnp.sum(p_meta, axis=-1, keepdims=True)
        o = (_dot(p_real.reshape(group * r, w).astype(BF16), vn)
             + _dot(p_meta.reshape(group * r, wm).astype(BF16), mv_ref[:, n * D_HEAD:(n + 1) * D_HEAD]))
        o = o / denom.reshape(group * r, 1)
        for g in range(group):
            o_ref[0, :, (n * group + g) * D_HEAD:(n * group + g + 1) * D_HEAD] = (
                o[g * r:(g + 1) * r].astype(o_ref.dtype))


def _dsa_prompt_body(*refs, topk, key_chunk, blocks_per_width):
    step = blocks_per_width * refs[0].shape[1]
    width_id = pl.program_id(1) // blocks_per_width
    for v in range(refs[3].shape[1] // step):
        pl.when(width_id == v)(functools.partial(
            _dsa_prompt_block, *refs, topk=topk, key_chunk=min(key_chunk, step), w=(v + 1) * step))


def _dsa_prompt(aq, iq_hm, iw, kb, vb, ikb, mk, mv, mik, topk):
    b, s, dq = aq.shape
    r = Q_BLOCK
    nqb = s // r
    wm = mk.shape[0]
    blocks_per_width = 4 if nqb % 4 == 0 else 1
    return pl.pallas_call(
        functools.partial(_dsa_prompt_body, topk=topk, key_chunk=512, blocks_per_width=blocks_per_width),
        grid=(b, nqb),
        in_specs=[pl.BlockSpec((1, r, dq), lambda bi, j: (bi, j, 0)),
                  pl.BlockSpec((N_IDX_HEADS, r, D_IDX), lambda bi, j: (0, bi * nqb + j, 0)),
                  pl.BlockSpec((1, r, N_IDX_HEADS), lambda bi, j: (bi, j, 0)),
                  pl.BlockSpec((1, s, _KV_W), lambda bi, j: (bi, 0, 0)),
                  pl.BlockSpec((1, s, _KV_W), lambda bi, j: (bi, 0, 0)),
                  pl.BlockSpec((1, s, D_IDX), lambda bi, j: (bi, 0, 0)),
                  pl.BlockSpec(mk.shape, lambda bi, j: (0, 0)),
                  pl.BlockSpec(mv.shape, lambda bi, j: (0, 0)),
                  pl.BlockSpec(mik.shape, lambda bi, j: (0, 0))],
        out_specs=pl.BlockSpec((1, r, dq), lambda bi, j: (bi, j, 0)),
        out_shape=jax.ShapeDtypeStruct((b, s, dq), BF16),
        scratch_shapes=[pltpu.VMEM((r, s + wm), I32), pltpu.VMEM((r, s + wm), F32)],
        compiler_params=_cparams(("parallel", "arbitrary")),
        name="dsa_prompt",
    )(aq, iq_hm, iw, kb, vb, ikb, mk, mv, mik)


def _page_specs(n_pages, rows, width):
    return [pl.BlockSpec((1, rows, width), functools.partial(lambda bi, pt, p: (pt[bi, p], 0, 0), p=p))
            for p in range(n_pages)]


def _sample_keys_body(pt_ref, iq_ref, iw_ref, ikn_ref, *rest, n_pages):
    pages = rest[:n_pages]
    key_ref = rest[n_pages]
    nq = key_ref.shape[1]
    iq_stack = iq_ref[0].reshape(N_IDX_HEADS * nq, D_IDX).astype(BF16)
    iw = iw_ref[0]
    iw_cols = [iw[h * nq:(h + 1) * nq] for h in range(N_IDX_HEADS)]
    all_valid = jnp.full((nq, PAGE_SIZE), True)
    for p in range(n_pages):
        key_ref[0, :, p * PAGE_SIZE:(p + 1) * PAGE_SIZE] = _indexer_keys(
            iq_stack, iw_cols, pages[p][0].astype(BF16), all_valid, keys_transposed=True)
    wn = ikn_ref.shape[1]
    col = lax.broadcasted_iota(I32, (nq, wn), 1)
    row = lax.broadcasted_iota(I32, (nq, wn), 0)
    key_ref[0, :, n_pages * PAGE_SIZE:n_pages * PAGE_SIZE + wn] = _indexer_keys(
        iq_stack, iw_cols, ikn_ref[0], col <= row)


def _sample_keys(page_table, iq4, iw_col, ik_new, cache_kidx):
    db, _, nq, _ = iq4.shape
    n_pages = page_table.shape[1]
    wn = ik_new.shape[1]
    wt = n_pages * PAGE_SIZE + wn
    grid_spec = pltpu.PrefetchScalarGridSpec(
        num_scalar_prefetch=1,
        grid=(db,),
        in_specs=[pl.BlockSpec((1, N_IDX_HEADS, nq, D_IDX), lambda bi, pt: (bi, 0, 0, 0)),
                  pl.BlockSpec((1, N_IDX_HEADS * nq, 1), lambda bi, pt: (bi, 0, 0)),
                  pl.BlockSpec((1, wn, D_IDX), lambda bi, pt: (bi, 0, 0))]
        + _page_specs(n_pages, D_IDX, PAGE_SIZE),
        out_specs=pl.BlockSpec((1, nq, wt), lambda bi, pt: (bi, 0, 0)),
    )
    return pl.pallas_call(
        functools.partial(_sample_keys_body, n_pages=n_pages),
        grid_spec=grid_spec,
        out_shape=jax.ShapeDtypeStruct((db, nq, wt), I32),
        compiler_params=_cparams(("arbitrary",)),
        name="sample_keys",
    )(page_table, iq4, iw_col, ik_new, *([cache_kidx] * n_pages))


def _select_body(key_ref, mask_ref, *, topk):
    _selection_mask(key_ref, mask_ref, topk)


def _select_mask(keys, topk, tr):
    rows, wt = keys.shape
    return pl.pallas_call(
        functools.partial(_select_body, topk=topk),
        grid=(rows // tr,),
        in_specs=[pl.BlockSpec((tr, wt), lambda i: (i, 0))],
        out_specs=pl.BlockSpec((tr, wt), lambda i: (i, 0)),
        out_shape=jax.ShapeDtypeStruct((rows, wt), F32),
        compiler_params=_cparams(("parallel",)),
        name="select_mask",
    )(keys)


def _sample_attn_body(pt_ref, q_ref, mask_ref, kn_ref, vn_ref, *rest, n_pages):
    kpages = rest[:n_pages]
    vpages = rest[n_pages:2 * n_pages]
    o_ref = rest[2 * n_pages]
    n_heads, nq = q_ref.shape[1], q_ref.shape[2]
    group = n_heads // N_KV_HEADS
    q = q_ref[0].reshape(n_heads * nq, D_HEAD).astype(BF16)
    wn = kn_ref.shape[1]
    chunks = [(p * PAGE_SIZE, PAGE_SIZE) for p in range(n_pages)] + [(n_pages * PAGE_SIZE, wn)]
    for n in range(N_KV_HEADS):
        qn = q[n * group * nq:(n + 1) * group * nq]
        hs = slice(n * D_HEAD, (n + 1) * D_HEAD)
        head_rows = pl.ds(n, PAGE_SIZE, stride=N_KV_HEADS)
        ks = [kpages[p][0, head_rows, :].astype(BF16) for p in range(n_pages)] + [kn_ref[0, :, hs]]
        vs = [vpages[p][0, head_rows, :].astype(BF16) for p in range(n_pages)] + [vn_ref[0, :, hs]]
        scores = []
        for (c0, cw), kc in zip(chunks, ks):
            mk = mask_ref[0, :, c0:c0 + cw]
            scores.append((_dot_nt(qn, kc).reshape(group, nq, cw) + mk[None]).reshape(group * nq, cw))
        m = functools.reduce(jnp.maximum, [jnp.max(sc, axis=-1, keepdims=True) for sc in scores])
        ps = [jnp.exp(sc - m) for sc in scores]
        denom = functools.reduce(jnp.add, [jnp.sum(pc, axis=-1, keepdims=True) for pc in ps])
        o = functools.reduce(jnp.add, [_dot(pc.astype(BF16), vc) for pc, vc in zip(ps, vs)])
        o_ref[0, n * group:(n + 1) * group] = (o / denom).reshape(group, nq, D_HEAD)


def _sample_attn(page_table, q4, mask, k_new, v_new, cache_k, cache_v):
    db, n_heads, nq, _ = q4.shape
    n_pages = page_table.shape[1]
    wn = k_new.shape[1]
    wt = mask.shape[2]
    grid_spec = pltpu.PrefetchScalarGridSpec(
        num_scalar_prefetch=1,
        grid=(db,),
        in_specs=[pl.BlockSpec((1, n_heads, nq, D_HEAD), lambda bi, pt: (bi, 0, 0, 0)),
                  pl.BlockSpec((1, nq, wt), lambda bi, pt: (bi, 0, 0)),
                  pl.BlockSpec((1, wn, _KV_W), lambda bi, pt: (bi, 0, 0)),
                  pl.BlockSpec((1, wn, _KV_W), lambda bi, pt: (bi, 0, 0))]
        + 2 * _page_specs(n_pages, PAGE_SIZE * N_KV_HEADS, D_HEAD),
        out_specs=pl.BlockSpec((1, n_heads, nq, D_HEAD), lambda bi, pt: (bi, 0, 0, 0)),
    )
    return pl.pallas_call(
        functools.partial(_sample_attn_body, n_pages=n_pages),
        grid_spec=grid_spec,
        out_shape=jax.ShapeDtypeStruct((db, n_heads, nq, D_HEAD), F32),
        compiler_params=_cparams(("arbitrary",)),
        name="sample_attn",
    )(page_table, q4, mask, k_new, v_new, *([cache_k] * n_pages), *([cache_v] * n_pages))


def _merge_body(oret_ref, rg_ref, odsa_ref, gates_ref, h_ref, wr_ref, wd_ref, wo_ref, nw_ref,
                h1_ref, hn_ref):
    d = h_ref.shape[1]
    rg = rg_ref[...].astype(F32)
    a = oret_ref[...].astype(F32) * (rg / (1.0 + jnp.exp(-rg)))
    br_ret = _dot(a.astype(BF16), wr_ref[...])
    br_dsa = _dot(odsa_ref[...], wd_ref[...])
    mix = gates_ref[:, :d].astype(F32) * br_ret + gates_ref[:, d:].astype(F32) * br_dsa
    h1 = h_ref[...] + _dot(mix.astype(BF16), wo_ref[...])
    h1_ref[...] = h1
    hn_ref[...] = (_rms(h1) * nw_ref[...]).astype(hn_ref.dtype)


def _merge(oret, vg, odsa, gates, h, w_ret, w_dsa, w_out, norm_w, tm):
    t, d = h.shape
    rw = oret.shape[1]
    resident = lambda a: pl.BlockSpec(a.shape, lambda i: (0, 0))
    return pl.pallas_call(
        _merge_body,
        grid=(t // tm,),
        in_specs=[pl.BlockSpec((tm, rw), lambda i: (i, 0)),
                  pl.BlockSpec((tm, rw), lambda i: (i, 1)),
                  pl.BlockSpec((tm, d), lambda i: (i, 0)),
                  pl.BlockSpec((tm, 2 * d), lambda i: (i, 0)),
                  pl.BlockSpec((tm, d), lambda i: (i, 0)),
                  resident(w_ret), resident(w_dsa), resident(w_out), resident(norm_w)],
        out_specs=[pl.BlockSpec((tm, d), lambda i: (i, 0)), pl.BlockSpec((tm, d), lambda i: (i, 0))],
        out_shape=[jax.ShapeDtypeStruct((t, d), F32), jax.ShapeDtypeStruct((t, d), BF16)],
        compiler_params=_cparams(("parallel",)),
        name="merge",
    )(oret, vg, odsa, gates, h, w_ret, w_dsa, w_out, norm_w)


def _ffn_body(hn_ref, h1_ref, wa_ref, wb_ref, wo_ref, y_ref, acc_ref):
    f = pl.program_id(1)
    a = _dot(hn_ref[...], wa_ref[...])
    b = _dot(hn_ref[...], wb_ref[...])
    g = (a / (1.0 + jnp.exp(-a)) * b).astype(BF16)
    part = _dot(g, wo_ref[...])

    @pl.when(f == 0)
    def _():
        acc_ref[...] = h1_ref[...] + part

    @pl.when(f != 0)
    def _():
        acc_ref[...] = acc_ref[...] + part

    @pl.when(f == pl.num_programs(1) - 1)
    def _():
        y_ref[...] = acc_ref[...]


def _ffn(hn, h1, w_in, w_out, tm, tf):
    t, d = h1.shape
    dff = w_out.shape[0]
    nf = dff // tf
    return pl.pallas_call(
        _ffn_body,
        grid=(t // tm, nf),
        in_specs=[pl.BlockSpec((tm, d), lambda i, f: (i, 0)),
                  pl.BlockSpec((tm, d), lambda i, f: (i, 0)),
                  pl.BlockSpec((d, tf), lambda i, f: (0, f)),
                  pl.BlockSpec((d, tf), lambda i, f: (0, nf + f)),
                  pl.BlockSpec((tf, d), lambda i, f: (f, 0))],
        out_specs=pl.BlockSpec((tm, d), lambda i, f: (i, 0)),
        out_shape=jax.ShapeDtypeStruct((t, d), F32),
        scratch_shapes=[pltpu.VMEM((tm, d), F32)],
        compiler_params=_cparams(("parallel", "arbitrary")),
        name="ffn",
    )(hn, h1, w_in, w_in, w_out)


def _rope_tables(pos):
    half = DK_RET // 2
    inv = ROPE_BASE ** (-jnp.arange(half, dtype=F32) / half)
    ang = pos.astype(F32)[:, None] * inv[None, :]
    return jnp.cos(ang), jnp.sin(ang)


def _row_tile(t, cap):
    tm = min(t, cap)
    assert t % tm == 0
    return tm


def _pad_rows(a, rows):
    return jnp.pad(a, [(0, rows - a.shape[0])] + [(0, 0)] * (a.ndim - 1))


def kernel(x_prompt, x_sample, cache_k, cache_v, cache_kidx, state_ret, page_table, meta_tokens,
           norm_mix_w, w_in, w_ret_proj, dsa_q_norm_w, dsa_k_norm_w, idx_k_norm_w, idx_k_norm_b,
           w_dsa_proj, w_out, norm_ffn_w, w_ffn_in, w_ffn_out):
    assert w_in.shape[0] == 1, "single-layer trunk"
    b, seq, d = x_prompt.shape
    db, dseq, _ = x_sample.shape
    n_meta = meta_tokens.shape[0]
    assert n_meta == N_META
    h_ret = d // DK_RET
    n_heads = d // D_HEAD
    rqk_w, rv_w = h_ret * DK_RET, h_ret * DV_RET
    n_pages = page_table.shape[1]
    past = n_pages * PAGE_SIZE
    n_pool = cache_k.shape[1]

    wi = w_in[0]
    c0 = 0
    w_rqk = wi[:, c0:c0 + 2 * rqk_w].astype(BF16); c0 += 2 * rqk_w
    w_rvg = wi[:, c0:c0 + 2 * rv_w].astype(BF16); c0 += 2 * rv_w
    w_aq = wi[:, c0:c0 + n_heads * D_HEAD].astype(BF16); c0 += n_heads * D_HEAD
    w_ak = wi[:, c0:c0 + _KV_W]; c0 += _KV_W
    w_av = wi[:, c0:c0 + _KV_W]; c0 += _KV_W
    w_iq = wi[:, c0:c0 + _IQ_W]; c0 += _IQ_W
    w_ik = wi[:, c0:c0 + D_IDX]; c0 += D_IDX
    w_iw = wi[:, c0:c0 + N_IDX_HEADS]; c0 += N_IDX_HEADS
    w_gz = wi[:, c0:c0 + 2 * d].astype(BF16); c0 += 2 * d
    assert c0 == wi.shape[1]
    zpad = lambda n: jnp.zeros((d, n), F32)
    w_small = jnp.concatenate([w_ak, w_av, w_iq, w_ik, zpad(LANES - D_IDX), w_iw, zpad(LANES - N_IDX_HEADS)],
                              axis=1).astype(BF16)
    assert w_small.shape[1] == _S_END
    q_gain = dsa_q_norm_w[0].reshape(1, D_HEAD)
    k_gain = dsa_k_norm_w[0].reshape(1, D_HEAD)
    ln_w = idx_k_norm_w[0].reshape(1, D_IDX)
    ln_b = idx_k_norm_b[0].reshape(1, D_IDX)
    w_ret_b = w_ret_proj[0].astype(BF16)
    w_dsa_b = w_dsa_proj[0].astype(BF16)
    w_out_b = w_out[0].astype(BF16)
    w_ffn_in_b = w_ffn_in[0].astype(BF16)
    w_ffn_out_b = w_ffn_out[0].astype(BF16)
    ffn_gain = norm_ffn_w[0].reshape(1, d)

    def project(x2d, pos, tm, want_queries):
        xn = _rmsnorm_bf16(x2d, norm_mix_w[0], tm)
        cos, sin = _rope_tables(pos)
        reps = max(1, tm // pos.shape[0])
        cos, sin = jnp.tile(cos, (reps, 1)), jnp.tile(sin, (reps, 1))
        out = {}
        out["rqk"] = _proj_rotary(xn, w_rqk, cos, sin, tm, rqk_w)
        out["rvg"] = _proj_simple(_proj_plain_body, xn, w_rvg, tm, 1024, "proj_rvg")
        (out["k"], out["v"], out["kb"], out["vb"], out["iq"], out["ik"], out["ikb"], out["iw"]) = _proj_small(
            xn, w_small, k_gain, ln_w, ln_b, tm)
        if want_queries:
            out["aq"] = _proj_simple(_proj_headnorm_body, xn, w_aq, tm, n_heads * D_HEAD, "proj_aq", (q_gain,))
            out["gates"] = _proj_simple(_proj_sigmoid_body, xn, w_gz, tm, 1024, "proj_gates")
        return out

    def post(x2d, oret, rvg, odsa, gates, tm):
        h1, hn = _merge(oret, rvg, odsa, gates, x2d, w_ret_b, w_dsa_b, w_out_b, ffn_gain, tm)
        dff = w_ffn_out_b.shape[0]
        tf = dff // 2 if (dff // 2) % LANES == 0 else dff
        return _ffn(hn, h1, w_ffn_in_b, w_ffn_out_b, tm, tf)

    pm = project(meta_tokens, jnp.arange(n_meta), n_meta, False)
    s_zero = jnp.zeros((1, h_ret, DK_RET, DV_RET), F32)
    _, s_meta = _retention(pm["rqk"][None], pm["rvg"][None], s_zero, n_meta, 1)

    tp = b * seq
    xp = x_prompt.reshape(tp, d)
    tm_p = _row_tile(seq, 512)
    pp = project(xp, n_meta + jnp.arange(seq), tm_p, True)
    oret_p, s_prompt = _retention(pp["rqk"].reshape(b, seq, -1), pp["rvg"].reshape(b, seq, -1),
                                  s_meta, RET_CHUNK, 1)
    topk_p = min(INDEX_TOPK, seq // 4)
    odsa_p = _dsa_prompt(pp["aq"].reshape(b, seq, -1), pp["iq"], pp["iw"].reshape(b, seq, -1),
                         pp["kb"].reshape(b, seq, -1), pp["vb"].reshape(b, seq, -1),
                         pp["ikb"].reshape(b, seq, -1),
                         _pad_rows(pm["kb"], LANES), _pad_rows(pm["vb"], LANES), _pad_rows(pm["ikb"], LANES),
                         topk_p)
    y_prompt = post(xp, oret_p.reshape(tp, -1), pp["rvg"], odsa_p.reshape(tp, -1), pp["gates"], tm_p)

    ts = db * dseq
    xs = x_sample.reshape(ts, d)
    tm_s = _row_tile(ts, 512)
    ps = project(xs, past + jnp.arange(dseq), tm_s, True)
    oret_s, s_sample = _retention(ps["rqk"].reshape(db, dseq, -1), ps["rvg"].reshape(db, dseq, -1),
                                  state_ret[0], dseq, h_ret)
    topk_s = min(INDEX_TOPK, (past + dseq) // 4)
    iq4 = ps["iq"].reshape(N_IDX_HEADS, db, dseq, D_IDX).transpose(1, 0, 2, 3).astype(F32)
    iw_col = ps["iw"].reshape(db, dseq, N_IDX_HEADS).transpose(0, 2, 1).reshape(db, N_IDX_HEADS * dseq, 1)
    pad_new = lambda a: jnp.pad(a.reshape(db, dseq, -1), ((0, 0), (0, LANES - dseq), (0, 0)))
    kidx_pages = jnp.swapaxes(cache_kidx[0], 1, 2)
    k_pages = cache_k[0].reshape(n_pool, PAGE_SIZE * N_KV_HEADS, D_HEAD)
    v_pages = cache_v[0].reshape(n_pool, PAGE_SIZE * N_KV_HEADS, D_HEAD)
    keys_s = _sample_keys(page_table, iq4, iw_col, pad_new(ps["ikb"]), kidx_pages)
    wt = keys_s.shape[2]
    mask_s = _select_mask(keys_s.reshape(ts, wt), topk_s, _row_tile(ts, 256)).reshape(db, dseq, wt)
    q4 = ps["aq"].reshape(db, dseq, n_heads, D_HEAD).transpose(0, 2, 1, 3).astype(F32)
    o4 = _sample_attn(page_table, q4, mask_s, pad_new(ps["kb"]), pad_new(ps["vb"]), k_pages, v_pages)
    odsa_s = o4.transpose(0, 2, 1, 3).reshape(ts, n_heads * D_HEAD).astype(BF16)
    y_sample = post(xs, oret_s.reshape(ts, -1), ps["rvg"], odsa_s, ps["gates"], tm_s)

    def with_meta(meta, real, tail):
        m = jnp.broadcast_to(meta[None], (b,) + meta.shape)
        return jnp.concatenate([m, real.reshape(b, seq, -1)], axis=1).reshape((1, b, n_meta + seq) + tail)

    kv_tail = (N_KV_HEADS, D_HEAD)
    return (y_prompt.reshape(b, seq, d),
            y_sample.reshape(db, dseq, d),
            with_meta(pm["k"], pp["k"], kv_tail),
            with_meta(pm["v"], pp["v"], kv_tail),
            with_meta(pm["ik"], pp["ik"], (D_IDX,)),
            s_prompt[None],
            ps["k"].reshape((1, db, dseq) + kv_tail),
            ps["v"].reshape((1, db, dseq) + kv_tail),
            ps["ik"].reshape(1, db, dseq, D_IDX),
            s_sample[None])
```

```python
import functools

import numpy as np
import jax
import jax.numpy as jnp
from jax import lax
from jax.experimental import pallas as pl
from jax.experimental.pallas import tpu as pltpu

F32 = jnp.float32
BF16 = jnp.bfloat16
I32 = jnp.int32

N_META = 16
DK_RET = 256
DV_RET = 512
RET_CHUNK = 128
ROPE_BASE = 10000.0
D_HEAD = 128
N_KV_HEADS = 2
N_IDX_HEADS = 8
D_IDX = 64
INDEX_TOPK = 256
Q_BLOCK = 128
PAGE_SIZE = 128
NORM_EPS = 1e-6
IDX_SCALE = (N_IDX_HEADS ** -0.5) * (D_IDX ** -0.5)
LOG2_E = 1.4426950408889634

LANES = 128
VMEM_LIMIT = 56 * 1024 * 1024

KEY_NEG_INF = int(np.array(0xFF800000 ^ 0x7FFFFFFF, np.uint32).view(np.int32))
POS_PAD = 1 << 30


def _cparams(sem):
    return pltpu.CompilerParams(dimension_semantics=sem, vmem_limit_bytes=VMEM_LIMIT)


def _dot(a, b):
    return jnp.dot(a, b, preferred_element_type=F32)


def _dot_nt(a, b):
    return lax.dot_general(a, b, (((1,), (1,)), ((), ())), preferred_element_type=F32)


def _dot_tn(a, b):
    return lax.dot_general(a, b, (((0,), (0,)), ((), ())), preferred_element_type=F32)


def _rms(x):
    return x * lax.rsqrt(jnp.mean(x * x, axis=-1, keepdims=True) + NORM_EPS)


def _norm_body(x_ref, w_ref, o_ref):
    o_ref[...] = (_rms(x_ref[...]) * w_ref[...]).astype(o_ref.dtype)


def _rmsnorm_bf16(x, w, tm):
    t, d = x.shape
    return pl.pallas_call(
        _norm_body,
        grid=(t // tm,),
        in_specs=[pl.BlockSpec((tm, d), lambda i: (i, 0)), pl.BlockSpec((1, d), lambda i: (0, 0))],
        out_specs=pl.BlockSpec((tm, d), lambda i: (i, 0)),
        out_shape=jax.ShapeDtypeStruct((t, d), BF16),
        compiler_params=_cparams(("parallel",)),
        name="rmsnorm",
    )(x, w.reshape(1, d))


def _proj_rotary_body(x_ref, w_ref, cos_ref, sin_ref, o_ref, *, q_blocks):
    z = _dot(x_ref[...], w_ref[...])
    scale = jnp.where(pl.program_id(0) >= q_blocks, DK_RET ** -0.5, 1.0).astype(F32)
    cos = cos_ref[...]
    sin = sin_ref[...]
    half = DK_RET // 2
    for h in range(z.shape[1] // DK_RET):
        x1 = z[:, h * DK_RET:h * DK_RET + half]
        x2 = z[:, h * DK_RET + half:(h + 1) * DK_RET]
        o_ref[:, h * DK_RET:h * DK_RET + half] = ((x1 * cos - x2 * sin) * scale).astype(o_ref.dtype)
        o_ref[:, h * DK_RET + half:(h + 1) * DK_RET] = ((x1 * sin + x2 * cos) * scale).astype(o_ref.dtype)


def _proj_rotary(xn, w, cos, sin, tm, tn):
    t, d = xn.shape
    n = w.shape[1]
    pos_blocks = cos.shape[0] // tm
    return pl.pallas_call(
        functools.partial(_proj_rotary_body, q_blocks=(n // 2) // tn),
        grid=(n // tn, t // tm),
        in_specs=[pl.BlockSpec((tm, d), lambda j, i: (i, 0)),
                  pl.BlockSpec((d, tn), lambda j, i: (0, j)),
                  pl.BlockSpec((tm, DK_RET // 2), lambda j, i: (i % pos_blocks, 0)),
                  pl.BlockSpec((tm, DK_RET // 2), lambda j, i: (i % pos_blocks, 0))],
        out_specs=pl.BlockSpec((tm, tn), lambda j, i: (i, j)),
        out_shape=jax.ShapeDtypeStruct((t, n), BF16),
        compiler_params=_cparams(("parallel", "parallel")),
        name="proj_rotary",
    )(xn, w, cos, sin)


def _proj_plain_body(x_ref, w_ref, o_ref):
    o_ref[...] = _dot(x_ref[...], w_ref[...]).astype(o_ref.dtype)


def _proj_sigmoid_body(x_ref, w_ref, o_ref):
    z = _dot(x_ref[...], w_ref[...])
    o_ref[...] = (1.0 / (1.0 + jnp.exp(-z))).astype(o_ref.dtype)


def _proj_headnorm_body(x_ref, w_ref, g_ref, o_ref):
    z = _dot(x_ref[...], w_ref[...])
    g = g_ref[...] * (D_HEAD ** -0.5 * LOG2_E)
    for h in range(z.shape[1] // D_HEAD):
        o_ref[:, h * D_HEAD:(h + 1) * D_HEAD] = (_rms(z[:, h * D_HEAD:(h + 1) * D_HEAD]) * g).astype(o_ref.dtype)


def _proj_simple(body, xn, w, tm, tn, name, extra=()):
    t, d = xn.shape
    n = w.shape[1]
    extra_specs = [pl.BlockSpec(e.shape, lambda j, i: (0, 0)) for e in extra]
    return pl.pallas_call(
        body,
        grid=(n // tn, t // tm),
        in_specs=[pl.BlockSpec((tm, d), lambda j, i: (i, 0)),
                  pl.BlockSpec((d, tn), lambda j, i: (0, j))] + extra_specs,
        out_specs=pl.BlockSpec((tm, tn), lambda j, i: (i, j)),
        out_shape=jax.ShapeDtypeStruct((t, n), BF16),
        compiler_params=_cparams(("parallel", "parallel")),
        name=name,
    )(xn, w, *extra)


_KV_W = N_KV_HEADS * D_HEAD
_IQ_W = N_IDX_HEADS * D_IDX
_S_K = 0
_S_V = _KV_W
_S_IQ = 2 * _KV_W
_S_IK = _S_IQ + _IQ_W
_S_IW = _S_IK + LANES
_S_END = _S_IW + LANES


def _proj_small_body(x_ref, w_ref, kg_ref, lnw_ref, lnb_ref,
                     k_ref, v_ref, kb_ref, vb_ref, iq_ref, ik_ref, ikb_ref, iw_ref):
    z = _dot(x_ref[...], w_ref[...])
    kg = kg_ref[...]
    for h in range(N_KV_HEADS):
        kh = _rms(z[:, _S_K + h * D_HEAD:_S_K + (h + 1) * D_HEAD]) * kg
        k_ref[:, h * D_HEAD:(h + 1) * D_HEAD] = kh
        kb_ref[:, h * D_HEAD:(h + 1) * D_HEAD] = kh.astype(BF16)
    v = z[:, _S_V:_S_V + _KV_W]
    v_ref[...] = v
    for h in range(N_KV_HEADS):
        vb_ref[:, 2 * h * D_HEAD:(2 * h + 1) * D_HEAD] = v[:, h * D_HEAD:(h + 1) * D_HEAD].astype(BF16)
        vb_ref[:, (2 * h + 1) * D_HEAD:(2 * h + 2) * D_HEAD] = jnp.ones((v.shape[0], D_HEAD), BF16)
    for h in range(N_IDX_HEADS):
        iq_ref[h] = z[:, _S_IQ + h * D_IDX:_S_IQ + (h + 1) * D_IDX].astype(BF16)
    ik = z[:, _S_IK:_S_IK + D_IDX]
    mu = jnp.mean(ik, axis=-1, keepdims=True)
    var = jnp.mean(jnp.square(ik - mu), axis=-1, keepdims=True)
    ikn = (ik - mu) * lax.rsqrt(var + NORM_EPS) * lnw_ref[...] + lnb_ref[...]
    ik_ref[...] = ikn
    ikb_ref[...] = ikn.astype(BF16)
    iw_ref[...] = z[:, _S_IW:_S_IW + N_IDX_HEADS] * IDX_SCALE


def _proj_small(xn, w, k_gain, ln_w, ln_b, tm):
    t, d = xn.shape
    row = lambda width: pl.BlockSpec((tm, width), lambda i: (i, 0))
    full = lambda a: pl.BlockSpec(a.shape, lambda i: (0, 0))
    return pl.pallas_call(
        _proj_small_body,
        grid=(t // tm,),
        in_specs=[row(d), full(w), full(k_gain), full(ln_w), full(ln_b)],
        out_specs=[row(_KV_W), row(_KV_W), row(_KV_W), row(2 * _KV_W),
                   pl.BlockSpec((N_IDX_HEADS, tm, D_IDX), lambda i: (0, i, 0)),
                   row(D_IDX), row(D_IDX), row(N_IDX_HEADS)],
        out_shape=[jax.ShapeDtypeStruct((t, _KV_W), F32), jax.ShapeDtypeStruct((t, _KV_W), F32),
                   jax.ShapeDtypeStruct((t, _KV_W), BF16), jax.ShapeDtypeStruct((t, 2 * _KV_W), BF16),
                   jax.ShapeDtypeStruct((N_IDX_HEADS, t, D_IDX), BF16),
                   jax.ShapeDtypeStruct((t, D_IDX), F32), jax.ShapeDtypeStruct((t, D_IDX), BF16),
                   jax.ShapeDtypeStruct((t, N_IDX_HEADS), F32)],
        compiler_params=_cparams(("parallel",)),
        name="proj_small",
    )(xn, w, k_gain, ln_w, ln_b)


def _retention_body(q_ref, k_ref, v_ref, s_in_ref, dec_ref, qd_ref, kd_ref, sd_ref,
                    o_ref, s_out_ref, s_scr, *, hb):
    c = pl.program_id(2)

    @pl.when(c == 0)
    def _():
        s_scr[...] = s_in_ref[0]

    for h in range(hb):
        q = q_ref[0, :, h * DK_RET:(h + 1) * DK_RET]
        k = k_ref[0, :, h * DK_RET:(h + 1) * DK_RET]
        v = v_ref[0, :, h * DV_RET:(h + 1) * DV_RET]
        s = s_scr[h]
        scores = _dot_nt(q, k) * dec_ref[h]
        o = _dot(scores.astype(BF16), v) + _dot(q, s.astype(BF16)) * qd_ref[h]
        kd = (k.astype(F32) * kd_ref[h]).astype(BF16)
        s_scr[h] = s * sd_ref[h] + _dot_tn(kd, v)
        o_ref[0, :, h * DV_RET:(h + 1) * DV_RET] = _rms(o).astype(o_ref.dtype)

    @pl.when(c == pl.num_programs(2) - 1)
    def _():
        s_out_ref[0] = s_scr[...]


def _retention_tables(n_heads, c):
    lg = jnp.log1p(-jnp.exp2(-5.0 - jnp.arange(n_heads, dtype=F32)))
    i = jnp.arange(c, dtype=F32)
    diff = i[:, None] - i[None, :]
    decay = jnp.where(diff >= 0, jnp.exp(lg[:, None, None] * jnp.maximum(diff, 0.0)), 0.0)
    q_decay = jnp.exp(lg[:, None] * (i + 1.0))
    k_decay = jnp.exp(lg[:, None] * (c - 1.0 - i))
    s_decay = jnp.exp(lg * c)
    return decay, q_decay[:, :, None], k_decay[:, :, None], s_decay[:, None, None]


def _retention(qk, vg, s_in, chunk, hb):
    b, l, _ = qk.shape
    n_heads = s_in.shape[1]
    nhb = n_heads // hb
    per_batch_state = s_in.shape[0] != 1
    tables = _retention_tables(n_heads, chunk)
    return pl.pallas_call(
        functools.partial(_retention_body, hb=hb),
        grid=(b, nhb, l // chunk),
        in_specs=[pl.BlockSpec((1, chunk, hb * DK_RET), lambda bi, hi, ci: (bi, ci, hi)),
                  pl.BlockSpec((1, chunk, hb * DK_RET), lambda bi, hi, ci: (bi, ci, nhb + hi)),
                  pl.BlockSpec((1, chunk, hb * DV_RET), lambda bi, hi, ci: (bi, ci, hi)),
                  pl.BlockSpec((1, hb, DK_RET, DV_RET),
                               (lambda bi, hi, ci: (bi, hi, 0, 0)) if per_batch_state
                               else (lambda bi, hi, ci: (0, hi, 0, 0))),
                  pl.BlockSpec((hb, chunk, chunk), lambda bi, hi, ci: (hi, 0, 0)),
                  pl.BlockSpec((hb, chunk, 1), lambda bi, hi, ci: (hi, 0, 0)),
                  pl.BlockSpec((hb, chunk, 1), lambda bi, hi, ci: (hi, 0, 0)),
                  pl.BlockSpec((hb, 1, 1), lambda bi, hi, ci: (hi, 0, 0))],
        out_specs=[pl.BlockSpec((1, chunk, hb * DV_RET), lambda bi, hi, ci: (bi, ci, hi)),
                   pl.BlockSpec((1, hb, DK_RET, DV_RET), lambda bi, hi, ci: (bi, hi, 0, 0))],
        out_shape=[jax.ShapeDtypeStruct((b, l, n_heads * DV_RET), BF16),
                   jax.ShapeDtypeStruct((b, n_heads, DK_RET, DV_RET), F32)],
        scratch_shapes=[pltpu.VMEM((hb, DK_RET, DV_RET), F32)],
        compiler_params=_cparams(("parallel", "parallel", "arbitrary")),
        name="retention",
    )(qk, qk, vg, s_in, *tables)


def _float_key(x):
    b = lax.bitcast_convert_type(x + 0.0, I32)
    return jnp.where(b < 0, b ^ jnp.int32(0x7FFFFFFF), b)


def _count_where(key_ref, pred):
    r, w = key_ref.shape
    acc = jnp.zeros((r, LANES), F32)
    for c in range(w // LANES):
        col = c * LANES + lax.broadcasted_iota(I32, (1, LANES), 1)
        acc = acc + pred(key_ref[:, c * LANES:(c + 1) * LANES], col)
    return jnp.sum(acc, axis=-1, keepdims=True)


def _count_ge(key_ref, t):
    return _count_where(key_ref, lambda key, col: jnp.where(key >= t, 1.0, 0.0))


def _kth_largest_keys(key_refs, k):
    def first(ref):
        zero = jnp.zeros((ref.shape[0], 1), I32)
        return jnp.where(_count_ge(ref, zero) >= k, zero, jnp.full_like(zero, -2 ** 31))

    def body(i, bases):
        bit = jnp.left_shift(jnp.int32(1), 30 - i)
        return tuple(jnp.where(_count_ge(ref, base | bit) >= k, base | bit, base)
                     for ref, base in zip(key_refs, bases))

    return lax.fori_loop(0, 31, body, tuple(first(ref) for ref in key_refs))


def _selection_masks(key_refs, mask_refs, k):
    thrs = [jnp.maximum(t, KEY_NEG_INF + 1) for t in _kth_largest_keys(key_refs, k)]
    over = [jnp.max(jnp.where(_count_ge(ref, t) > k, 1.0, 0.0)) for ref, t in zip(key_refs, thrs)]
    tied = functools.reduce(jnp.maximum, over) > 0.0

    def store(mask_for):
        for ref, mref, t in zip(key_refs, mask_refs, thrs):
            mask_of = mask_for(ref, t)
            for c in range(ref.shape[1] // LANES):
                sl = slice(c * LANES, (c + 1) * LANES)
                col = c * LANES + lax.broadcasted_iota(I32, (1, LANES), 1)
                mref[:, sl] = mask_of(ref[:, sl], col).astype(mref.dtype)

    @pl.when(jnp.logical_not(tied))
    def _():
        store(lambda ref, t: lambda key, col: jnp.where(key >= t, 0.0, -jnp.inf))

    @pl.when(tied)
    def _():
        def mask_for(ref, t):
            need = k - _count_ge(ref, t + 1)
            limit = jnp.zeros_like(t)
            for bit in reversed(range(int(ref.shape[1]).bit_length())):
                cand = limit | (1 << bit)
                cnt = _count_where(ref, lambda key, col: jnp.where(key == t, jnp.where(col < cand, 1.0, 0.0), 0.0))
                limit = jnp.where(cnt <= need, cand, limit)
            return lambda key, col: jnp.where(key == t, jnp.where(col < limit, 0.0, -jnp.inf),
                                              jnp.where(key > t, 0.0, -jnp.inf))
        store(mask_for)


def _indexer_keys(iq_stack, iw_cols, ik, valid, keys_transposed=False):
    r = valid.shape[0]
    s = _dot(iq_stack, ik) if keys_transposed else _dot_nt(iq_stack, ik)
    acc = None
    for h in range(N_IDX_HEADS):
        t = jnp.maximum(s[h * r:(h + 1) * r], 0.0) * iw_cols[h]
        acc = t if acc is None else acc + t
    return jnp.where(valid, _float_key(acc), KEY_NEG_INF)


def _softmax_pv(s_chunks, v_chunks):
    m = functools.reduce(jnp.maximum, [jnp.max(sc, axis=-1, keepdims=True) for sc in s_chunks])
    acc = functools.reduce(jnp.add, [_dot(jnp.exp2((sc - m).astype(BF16)), vc)
                                     for sc, vc in zip(s_chunks, v_chunks)])
    return acc[:, :D_HEAD] / acc[:, D_HEAD:D_HEAD + 1]


def _dsa_prompt_group(aq_ref, iq_ref, iw_ref, k_ref, v_ref, ik_ref, mk_ref, mv_ref, mik_ref,
                      o_ref, key_scr, mask_scr, *, topk, key_chunk, sub, w):
    rows = aq_ref.shape[1]
    n_sub = rows // sub
    wm = mk_ref.shape[0]
    n_heads = aq_ref.shape[2] // D_HEAD
    group = n_heads // N_KV_HEADS
    row0 = pl.program_id(1) * rows

    def index_block(i, carry):
        r0 = pl.multiple_of(i * sub, sub)
        rs = pl.ds(r0, sub)
        qpos = N_META + row0 + r0 + lax.broadcasted_iota(I32, (sub, 1), 0)
        iq_stack = iq_ref[:, rs, :].reshape(N_IDX_HEADS * sub, D_IDX)
        iw = iw_ref[0, rs, :]
        iw_cols = [iw[:, h:h + 1] for h in range(N_IDX_HEADS)]
        mcol = lax.broadcasted_iota(I32, (1, wm), 1)
        key_scr[rs, :wm] = _indexer_keys(iq_stack, iw_cols, mik_ref[...],
                                         jnp.broadcast_to(mcol < N_META, (sub, wm)))
        for c0 in range(0, w, key_chunk):
            pos = N_META + c0 + lax.broadcasted_iota(I32, (1, key_chunk), 1)
            key_scr[rs, wm + c0:wm + c0 + key_chunk] = _indexer_keys(
                iq_stack, iw_cols, ik_ref[0, c0:c0 + key_chunk, :], pos <= qpos)
        return carry

    lax.fori_loop(0, n_sub, index_block, 0)

    blocks = [slice(i * sub, (i + 1) * sub) for i in range(n_sub)]
    _selection_masks([key_scr.at[bs, :wm + w] for bs in blocks],
                     [mask_scr.at[bs, :wm + w] for bs in blocks], topk)

    def attend_block(i, carry):
        rs = pl.ds(pl.multiple_of(i * sub, sub), sub)
        mask_meta = mask_scr[rs, :wm]
        mask_real = mask_scr[rs, wm:wm + w]
        for n in range(N_KV_HEADS):
            hs = slice(n * D_HEAD, (n + 1) * D_HEAD)
            qs = jnp.concatenate([aq_ref[0, rs, (n * group + g) * D_HEAD:(n * group + g + 1) * D_HEAD]
                                  for g in range(group)], axis=0)
            s_real = (_dot_nt(qs, k_ref[0, :w, hs]).reshape(group, sub, w) + mask_real[None])
            s_meta = (_dot_nt(qs, mk_ref[:, hs]).reshape(group, sub, wm) + mask_meta[None])
            vs = slice(2 * n * D_HEAD, (2 * n + 2) * D_HEAD)
            o = _softmax_pv([s_real.reshape(group * sub, w), s_meta.reshape(group * sub, wm)],
                            [v_ref[0, :w, vs], mv_ref[:, vs]])
            for g in range(group):
                o_ref[0, rs, (n * group + g) * D_HEAD:(n * group + g + 1) * D_HEAD] = (
                    o[g * sub:(g + 1) * sub].astype(o_ref.dtype))
        return carry

    lax.fori_loop(0, n_sub, attend_block, 0)


def _dsa_prompt_body(*refs, topk, key_chunk, sub):
    rows = refs[0].shape[1]
    for v in range(refs[3].shape[1] // rows):
        pl.when(pl.program_id(1) == v)(functools.partial(
            _dsa_prompt_group, *refs, topk=topk, key_chunk=min(key_chunk, rows), sub=sub, w=(v + 1) * rows))


def _dsa_prompt(aq, iq_hm, iw, kb, vb, ikb, mk, mv, mik, topk):
    b, s, dq = aq.shape
    sub = min(Q_BLOCK, s)
    rows = 4 * sub if s % (4 * sub) == 0 else sub
    ng = s // rows
    wm = mk.shape[0]
    return pl.pallas_call(
        functools.partial(_dsa_prompt_body, topk=topk, key_chunk=512, sub=sub),
        grid=(b, ng),
        in_specs=[pl.BlockSpec((1, rows, dq), lambda bi, j: (bi, j, 0)),
                  pl.BlockSpec((N_IDX_HEADS, rows, D_IDX), lambda bi, j: (0, bi * ng + j, 0)),
                  pl.BlockSpec((1, rows, N_IDX_HEADS), lambda bi, j: (bi, j, 0)),
                  pl.BlockSpec((1, s, _KV_W), lambda bi, j: (bi, 0, 0)),
                  pl.BlockSpec((1, s, 2 * _KV_W), lambda bi, j: (bi, 0, 0)),
                  pl.BlockSpec((1, s, D_IDX), lambda bi, j: (bi, 0, 0)),
                  pl.BlockSpec(mk.shape, lambda bi, j: (0, 0)),
                  pl.BlockSpec(mv.shape, lambda bi, j: (0, 0)),
                  pl.BlockSpec(mik.shape, lambda bi, j: (0, 0))],
        out_specs=pl.BlockSpec((1, rows, dq), lambda bi, j: (bi, j, 0)),
        out_shape=jax.ShapeDtypeStruct((b, s, dq), BF16),
        scratch_shapes=[pltpu.VMEM((rows, s + wm), I32), pltpu.VMEM((rows, s + wm), F32)],
        compiler_params=_cparams(("parallel", "arbitrary")),
        name="dsa_prompt",
    )(aq, iq_hm, iw, kb, vb, ikb, mk, mv, mik)


def _page_specs(n_pages, rows, width):
    return [pl.BlockSpec((1, rows, width), functools.partial(lambda bi, pt, p: (pt[bi, p], 0, 0), p=p))
            for p in range(n_pages)]


def _sample_keys_body(pt_ref, iq_ref, iw_ref, ikn_ref, *rest, n_pages):
    pages = rest[:n_pages]
    key_ref = rest[n_pages]
    nq = key_ref.shape[1]
    iq_stack = iq_ref[0].reshape(N_IDX_HEADS * nq, D_IDX).astype(BF16)
    iw = iw_ref[0]
    iw_cols = [iw[h * nq:(h + 1) * nq] for h in range(N_IDX_HEADS)]
    all_valid = jnp.full((nq, PAGE_SIZE), True)
    for p in range(n_pages):
        key_ref[0, :, p * PAGE_SIZE:(p + 1) * PAGE_SIZE] = _indexer_keys(
            iq_stack, iw_cols, pages[p][0].astype(BF16), all_valid, keys_transposed=True)
    wn = ikn_ref.shape[1]
    col = lax.broadcasted_iota(I32, (nq, wn), 1)
    row = lax.broadcasted_iota(I32, (nq, wn), 0)
    key_ref[0, :, n_pages * PAGE_SIZE:n_pages * PAGE_SIZE + wn] = _indexer_keys(
        iq_stack, iw_cols, ikn_ref[0], col <= row)


def _sample_keys(page_table, iq4, iw_col, ik_new, cache_kidx):
    db, _, nq, _ = iq4.shape
    n_pages = page_table.shape[1]
    wn = ik_new.shape[1]
    wt = n_pages * PAGE_SIZE + wn
    grid_spec = pltpu.PrefetchScalarGridSpec(
        num_scalar_prefetch=1,
        grid=(db,),
        in_specs=[pl.BlockSpec((1, N_IDX_HEADS, nq, D_IDX), lambda bi, pt: (bi, 0, 0, 0)),
                  pl.BlockSpec((1, N_IDX_HEADS * nq, 1), lambda bi, pt: (bi, 0, 0)),
                  pl.BlockSpec((1, wn, D_IDX), lambda bi, pt: (bi, 0, 0))]
        + _page_specs(n_pages, D_IDX, PAGE_SIZE),
        out_specs=pl.BlockSpec((1, nq, wt), lambda bi, pt: (bi, 0, 0)),
    )
    return pl.pallas_call(
        functools.partial(_sample_keys_body, n_pages=n_pages),
        grid_spec=grid_spec,
        out_shape=jax.ShapeDtypeStruct((db, nq, wt), I32),
        compiler_params=_cparams(("arbitrary",)),
        name="sample_keys",
    )(page_table, iq4, iw_col, ik_new, *([cache_kidx] * n_pages))


def _select_body(key_ref, mask_ref, *, topk):
    rows = key_ref.shape[0]
    groups = [slice(r0, min(r0 + Q_BLOCK, rows)) for r0 in range(0, rows, Q_BLOCK)]
    _selection_masks([key_ref.at[g] for g in groups], [mask_ref.at[g] for g in groups], topk)


def _select_mask(keys, topk, tr):
    rows, wt = keys.shape
    return pl.pallas_call(
        functools.partial(_select_body, topk=topk),
        grid=(rows // tr,),
        in_specs=[pl.BlockSpec((tr, wt), lambda i: (i, 0))],
        out_specs=pl.BlockSpec((tr, wt), lambda i: (i, 0)),
        out_shape=jax.ShapeDtypeStruct((rows, wt), F32),
        compiler_params=_cparams(("parallel",)),
        name="select_mask",
    )(keys)


def _sample_attn_body(pt_ref, q_ref, mask_ref, kn_ref, vn_ref, *rest, n_pages):
    kpages = rest[:n_pages]
    vpages = rest[n_pages:2 * n_pages]
    o_ref = rest[2 * n_pages]
    n_heads, nq = q_ref.shape[1], q_ref.shape[2]
    group = n_heads // N_KV_HEADS
    q = q_ref[0].reshape(n_heads * nq, D_HEAD).astype(BF16)
    wn = kn_ref.shape[1]
    chunks = [(p * PAGE_SIZE, PAGE_SIZE) for p in range(n_pages)] + [(n_pages * PAGE_SIZE, wn)]
    for n in range(N_KV_HEADS):
        qn = q[n * group * nq:(n + 1) * group * nq]
        hs = slice(n * D_HEAD, (n + 1) * D_HEAD)
        head_rows = pl.ds(n, PAGE_SIZE, stride=N_KV_HEADS)
        ks = [kpages[p][0, head_rows, :].astype(BF16) for p in range(n_pages)] + [kn_ref[0, :, hs]]
        vs = [vpages[p][0, head_rows, :].astype(BF16) for p in range(n_pages)] + [
            vn_ref[0, :, 2 * n * D_HEAD:(2 * n + 1) * D_HEAD]]
        scores = []
        for (c0, cw), kc in zip(chunks, ks):
            mk = mask_ref[0, :, c0:c0 + cw]
            scores.append((_dot_nt(qn, kc).reshape(group, nq, cw) + mk[None]).reshape(group * nq, cw))
        m = functools.reduce(jnp.maximum, [jnp.max(sc, axis=-1, keepdims=True) for sc in scores])
        ps = [jnp.exp2(sc - m) for sc in scores]
        denom = functools.reduce(jnp.add, [jnp.sum(pc, axis=-1, keepdims=True) for pc in ps])
        o = functools.reduce(jnp.add, [_dot(pc.astype(BF16), vc) for pc, vc in zip(ps, vs)])
        o_ref[0, n * group:(n + 1) * group] = (o / denom).reshape(group, nq, D_HEAD)


def _sample_attn(page_table, q4, mask, k_new, v_new, cache_k, cache_v):
    db, n_heads, nq, _ = q4.shape
    n_pages = page_table.shape[1]
    wn = k_new.shape[1]
    wt = mask.shape[2]
    grid_spec = pltpu.PrefetchScalarGridSpec(
        num_scalar_prefetch=1,
        grid=(db,),
        in_specs=[pl.BlockSpec((1, n_heads, nq, D_HEAD), lambda bi, pt: (bi, 0, 0, 0)),
                  pl.BlockSpec((1, nq, wt), lambda bi, pt: (bi, 0, 0)),
                  pl.BlockSpec((1, wn, _KV_W), lambda bi, pt: (bi, 0, 0)),
                  pl.BlockSpec((1, wn, 2 * _KV_W), lambda bi, pt: (bi, 0, 0))]
        + 2 * _page_specs(n_pages, PAGE_SIZE * N_KV_HEADS, D_HEAD),
        out_specs=pl.BlockSpec((1, n_heads, nq, D_HEAD), lambda bi, pt: (bi, 0, 0, 0)),
    )
    return pl.pallas_call(
        functools.partial(_sample_attn_body, n_pages=n_pages),
        grid_spec=grid_spec,
        out_shape=jax.ShapeDtypeStruct((db, n_heads, nq, D_HEAD), F32),
        compiler_params=_cparams(("arbitrary",)),
        name="sample_attn",
    )(page_table, q4, mask, k_new, v_new, *([cache_k] * n_pages), *([cache_v] * n_pages))


def _merge_body(oret_ref, rg_ref, odsa_ref, gates_ref, h_ref, wr_ref, wd_ref, wo_ref, nw_ref,
                h1_ref, hn_ref):
    d = h_ref.shape[1]
    rg = rg_ref[...].astype(F32)
    a = oret_ref[...].astype(F32) * (rg / (1.0 + jnp.exp(-rg)))
    br_ret = _dot(a.astype(BF16), wr_ref[...])
    br_dsa = _dot(odsa_ref[...], wd_ref[...])
    mix = gates_ref[:, :d].astype(F32) * br_ret + gates_ref[:, d:].astype(F32) * br_dsa
    h1 = h_ref[...] + _dot(mix.astype(BF16), wo_ref[...])
    h1_ref[...] = h1
    hn_ref[...] = (_rms(h1) * nw_ref[...]).astype(hn_ref.dtype)


def _merge(oret, vg, odsa, gates, h, w_ret, w_dsa, w_out, norm_w, tm):
    t, d = h.shape
    rw = oret.shape[1]
    resident = lambda a: pl.BlockSpec(a.shape, lambda i: (0, 0))
    return pl.pallas_call(
        _merge_body,
        grid=(t // tm,),
        in_specs=[pl.BlockSpec((tm, rw), lambda i: (i, 0)),
                  pl.BlockSpec((tm, rw), lambda i: (i, 1)),
                  pl.BlockSpec((tm, d), lambda i: (i, 0)),
                  pl.BlockSpec((tm, 2 * d), lambda i: (i, 0)),
                  pl.BlockSpec((tm, d), lambda i: (i, 0)),
                  resident(w_ret), resident(w_dsa), resident(w_out), resident(norm_w)],
        out_specs=[pl.BlockSpec((tm, d), lambda i: (i, 0)), pl.BlockSpec((tm, d), lambda i: (i, 0))],
        out_shape=[jax.ShapeDtypeStruct((t, d), F32), jax.ShapeDtypeStruct((t, d), BF16)],
        compiler_params=_cparams(("parallel",)),
        name="merge",
    )(oret, vg, odsa, gates, h, w_ret, w_dsa, w_out, norm_w)


def _ffn_body(hn_ref, h1_ref, wa_ref, wb_ref, wo_ref, y_ref, acc_ref):
    f = pl.program_id(1)
    a = _dot(hn_ref[...], wa_ref[...])
    b = _dot(hn_ref[...], wb_ref[...])
    g = (a / (1.0 + jnp.exp(-a)) * b).astype(BF16)
    part = _dot(g, wo_ref[...])

    @pl.when(f == 0)
    def _():
        acc_ref[...] = h1_ref[...] + part

    @pl.when(f != 0)
    def _():
        acc_ref[...] = acc_ref[...] + part

    @pl.when(f == pl.num_programs(1) - 1)
    def _():
        y_ref[...] = acc_ref[...]


def _ffn(hn, h1, w_in, w_out, tm, tf):
    t, d = h1.shape
    dff = w_out.shape[0]
    nf = dff // tf
    return pl.pallas_call(
        _ffn_body,
        grid=(t // tm, nf),
        in_specs=[pl.BlockSpec((tm, d), lambda i, f: (i, 0)),
                  pl.BlockSpec((tm, d), lambda i, f: (i, 0)),
                  pl.BlockSpec((d, tf), lambda i, f: (0, f)),
                  pl.BlockSpec((d, tf), lambda i, f: (0, nf + f)),
                  pl.BlockSpec((tf, d), lambda i, f: (f, 0))],
        out_specs=pl.BlockSpec((tm, d), lambda i, f: (i, 0)),
        out_shape=jax.ShapeDtypeStruct((t, d), F32),
        scratch_shapes=[pltpu.VMEM((tm, d), F32)],
        compiler_params=_cparams(("parallel", "arbitrary")),
        name="ffn",
    )(hn, h1, w_in, w_in, w_out)


def _rope_tables(pos):
    half = DK_RET // 2
    inv = ROPE_BASE ** (-jnp.arange(half, dtype=F32) / half)
    ang = pos.astype(F32)[:, None] * inv[None, :]
    return jnp.cos(ang), jnp.sin(ang)


def _row_tile(t, cap):
    tm = min(t, cap)
    assert t % tm == 0
    return tm


def _pad_rows(a, rows):
    return jnp.pad(a, [(0, rows - a.shape[0])] + [(0, 0)] * (a.ndim - 1))


def kernel(x_prompt, x_sample, cache_k, cache_v, cache_kidx, state_ret, page_table, meta_tokens,
           norm_mix_w, w_in, w_ret_proj, dsa_q_norm_w, dsa_k_norm_w, idx_k_norm_w, idx_k_norm_b,
           w_dsa_proj, w_out, norm_ffn_w, w_ffn_in, w_ffn_out):
    assert w_in.shape[0] == 1, "single-layer trunk"
    b, seq, d = x_prompt.shape
    db, dseq, _ = x_sample.shape
    n_meta = meta_tokens.shape[0]
    assert n_meta == N_META
    h_ret = d // DK_RET
    n_heads = d // D_HEAD
    rqk_w, rv_w = h_ret * DK_RET, h_ret * DV_RET
    n_pages = page_table.shape[1]
    past = n_pages * PAGE_SIZE
    n_pool = cache_k.shape[1]

    wi = w_in[0]
    c0 = 0
    w_rqk = wi[:, c0:c0 + 2 * rqk_w].astype(BF16); c0 += 2 * rqk_w
    w_rvg = wi[:, c0:c0 + 2 * rv_w].astype(BF16); c0 += 2 * rv_w
    w_aq = wi[:, c0:c0 + n_heads * D_HEAD].astype(BF16); c0 += n_heads * D_HEAD
    w_ak = wi[:, c0:c0 + _KV_W]; c0 += _KV_W
    w_av = wi[:, c0:c0 + _KV_W]; c0 += _KV_W
    w_iq = wi[:, c0:c0 + _IQ_W]; c0 += _IQ_W
    w_ik = wi[:, c0:c0 + D_IDX]; c0 += D_IDX
    w_iw = wi[:, c0:c0 + N_IDX_HEADS]; c0 += N_IDX_HEADS
    w_gz = wi[:, c0:c0 + 2 * d].astype(BF16); c0 += 2 * d
    assert c0 == wi.shape[1]
    zpad = lambda n: jnp.zeros((d, n), F32)
    w_small = jnp.concatenate([w_ak, w_av, w_iq, w_ik, zpad(LANES - D_IDX), w_iw, zpad(LANES - N_IDX_HEADS)],
                              axis=1).astype(BF16)
    assert w_small.shape[1] == _S_END
    q_gain = dsa_q_norm_w[0].reshape(1, D_HEAD)
    k_gain = dsa_k_norm_w[0].reshape(1, D_HEAD)
    ln_w = idx_k_norm_w[0].reshape(1, D_IDX)
    ln_b = idx_k_norm_b[0].reshape(1, D_IDX)
    w_ret_b = w_ret_proj[0].astype(BF16)
    w_dsa_b = w_dsa_proj[0].astype(BF16)
    w_out_b = w_out[0].astype(BF16)
    w_ffn_in_b = w_ffn_in[0].astype(BF16)
    w_ffn_out_b = w_ffn_out[0].astype(BF16)
    ffn_gain = norm_ffn_w[0].reshape(1, d)

    def project(x2d, pos, tm, want_queries):
        xn = _rmsnorm_bf16(x2d, norm_mix_w[0], tm)
        cos, sin = _rope_tables(pos)
        reps = max(1, tm // pos.shape[0])
        cos, sin = jnp.tile(cos, (reps, 1)), jnp.tile(sin, (reps, 1))
        out = {}
        out["rqk"] = _proj_rotary(xn, w_rqk, cos, sin, tm, rqk_w)
        out["rvg"] = _proj_simple(_proj_plain_body, xn, w_rvg, tm, 1024, "proj_rvg")
        (out["k"], out["v"], out["kb"], out["vb"], out["iq"], out["ik"], out["ikb"], out["iw"]) = _proj_small(
            xn, w_small, k_gain, ln_w, ln_b, tm)
        if want_queries:
            out["aq"] = _proj_simple(_proj_headnorm_body, xn, w_aq, tm, n_heads * D_HEAD, "proj_aq", (q_gain,))
            out["gates"] = _proj_simple(_proj_sigmoid_body, xn, w_gz, tm, 1024, "proj_gates")
        return out

    def post(x2d, oret, rvg, odsa, gates, tm):
        h1, hn = _merge(oret, rvg, odsa, gates, x2d, w_ret_b, w_dsa_b, w_out_b, ffn_gain, tm)
        dff = w_ffn_out_b.shape[0]
        tf = dff // 2 if (dff // 2) % LANES == 0 else dff
        return _ffn(hn, h1, w_ffn_in_b, w_ffn_out_b, tm, tf)

    pm = project(meta_tokens, jnp.arange(n_meta), n_meta, False)
    s_zero = jnp.zeros((1, h_ret, DK_RET, DV_RET), F32)
    _, s_meta = _retention(pm["rqk"][None], pm["rvg"][None], s_zero, n_meta, 1)

    tp = b * seq
    xp = x_prompt.reshape(tp, d)
    tm_p = _row_tile(seq, 512)
    pp = project(xp, n_meta + jnp.arange(seq), tm_p, True)
    oret_p, s_prompt = _retention(pp["rqk"].reshape(b, seq, -1), pp["rvg"].reshape(b, seq, -1),
                                  s_meta, RET_CHUNK, h_ret)
    topk_p = min(INDEX_TOPK, seq // 4)
    odsa_p = _dsa_prompt(pp["aq"].reshape(b, seq, -1), pp["iq"], pp["iw"].reshape(b, seq, -1),
                         pp["kb"].reshape(b, seq, -1), pp["vb"].reshape(b, seq, -1),
                         pp["ikb"].reshape(b, seq, -1),
                         _pad_rows(pm["kb"], LANES), _pad_rows(pm["vb"], LANES), _pad_rows(pm["ikb"], LANES),
                         topk_p)
    y_prompt = post(xp, oret_p.reshape(tp, -1), pp["rvg"], odsa_p.reshape(tp, -1), pp["gates"], tm_p)

    ts = db * dseq
    xs = x_sample.reshape(ts, d)
    tm_s = _row_tile(ts, 512)
    ps = project(xs, past + jnp.arange(dseq), tm_s, True)
    oret_s, s_sample = _retention(ps["rqk"].reshape(db, dseq, -1), ps["rvg"].reshape(db, dseq, -1),
                                  state_ret[0], dseq, h_ret)
    topk_s = min(INDEX_TOPK, (past + dseq) // 4)
    iq4 = ps["iq"].reshape(N_IDX_HEADS, db, dseq, D_IDX).transpose(1, 0, 2, 3).astype(F32)
    iw_col = ps["iw"].reshape(db, dseq, N_IDX_HEADS).transpose(0, 2, 1).reshape(db, N_IDX_HEADS * dseq, 1)
    pad_new = lambda a: jnp.pad(a.reshape(db, dseq, -1), ((0, 0), (0, LANES - dseq), (0, 0)))
    kidx_pages = jnp.swapaxes(cache_kidx[0], 1, 2)
    k_pages = cache_k[0].reshape(n_pool, PAGE_SIZE * N_KV_HEADS, D_HEAD)
    v_pages = cache_v[0].reshape(n_pool, PAGE_SIZE * N_KV_HEADS, D_HEAD)
    keys_s = _sample_keys(page_table, iq4, iw_col, pad_new(ps["ikb"]), kidx_pages)
    wt = keys_s.shape[2]
    mask_s = _select_mask(keys_s.reshape(ts, wt), topk_s, _row_tile(ts, 256)).reshape(db, dseq, wt)
    q4 = ps["aq"].reshape(db, dseq, n_heads, D_HEAD).transpose(0, 2, 1, 3).astype(F32)
    o4 = _sample_attn(page_table, q4, mask_s, pad_new(ps["kb"]), pad_new(ps["vb"]), k_pages, v_pages)
    odsa_s = o4.transpose(0, 2, 1, 3).reshape(ts, n_heads * D_HEAD).astype(BF16)
    y_sample = post(xs, oret_s.reshape(ts, -1), ps["rvg"], odsa_s, ps["gates"], tm_s)

    def with_meta(meta, real, tail):
        m = jnp.broadcast_to(meta[None], (b,) + meta.shape)
        return jnp.concatenate([m, real.reshape(b, seq, -1)], axis=1).reshape((1, b, n_meta + seq) + tail)

    kv_tail = (N_KV_HEADS, D_HEAD)
    return (y_prompt.reshape(b, seq, d),
            y_sample.reshape(db, dseq, d),
            with_meta(pm["k"], pp["k"], kv_tail),
            with_meta(pm["v"], pp["v"], kv_tail),
            with_meta(pm["ik"], pp["ik"], (D_IDX,)),
            s_prompt[None],
            ps["k"].reshape((1, db, dseq) + kv_tail),
            ps["v"].reshape((1, db, dseq) + kv_tail),
            ps["ik"].reshape(1, db, dseq, D_IDX),
            s_sample[None])
```

```python
import functools

import numpy as np
import jax
import jax.numpy as jnp
from jax import lax
from jax.experimental import pallas as pl
from jax.experimental.pallas import tpu as pltpu

F32 = jnp.float32
BF16 = jnp.bfloat16
I32 = jnp.int32

N_META = 16
DK_RET = 256
DV_RET = 512
RET_CHUNK = 128
ROPE_BASE = 10000.0
D_HEAD = 128
N_KV_HEADS = 2
N_IDX_HEADS = 8
D_IDX = 64
INDEX_TOPK = 256
Q_BLOCK = 128
PAGE_SIZE = 128
NORM_EPS = 1e-6
IDX_SCALE = (N_IDX_HEADS ** -0.5) * (D_IDX ** -0.5)
LOG2_E = 1.4426950408889634

LANES = 128
VMEM_LIMIT = 56 * 1024 * 1024

KEY_NEG_INF = int(np.array(0xFF800000 ^ 0x7FFFFFFF, np.uint32).view(np.int32))
POS_PAD = 1 << 30


def _cparams(sem):
    return pltpu.CompilerParams(dimension_semantics=sem, vmem_limit_bytes=VMEM_LIMIT)


def _dot(a, b):
    return jnp.dot(a, b, preferred_element_type=F32)


def _dot_nt(a, b):
    return lax.dot_general(a, b, (((1,), (1,)), ((), ())), preferred_element_type=F32)


def _dot_tn(a, b):
    return lax.dot_general(a, b, (((0,), (0,)), ((), ())), preferred_element_type=F32)


def _rms(x):
    return x * lax.rsqrt(jnp.mean(x * x, axis=-1, keepdims=True) + NORM_EPS)


def _norm_body(x_ref, w_ref, o_ref):
    o_ref[...] = (_rms(x_ref[...]) * w_ref[...]).astype(o_ref.dtype)


def _rmsnorm_bf16(x, w, tm):
    t, d = x.shape
    return pl.pallas_call(
        _norm_body,
        grid=(t // tm,),
        in_specs=[pl.BlockSpec((tm, d), lambda i: (i, 0)), pl.BlockSpec((1, d), lambda i: (0, 0))],
        out_specs=pl.BlockSpec((tm, d), lambda i: (i, 0)),
        out_shape=jax.ShapeDtypeStruct((t, d), BF16),
        compiler_params=_cparams(("parallel",)),
        name="rmsnorm",
    )(x, w.reshape(1, d))


def _proj_rotary_body(x_ref, w_ref, cos_ref, sin_ref, o_ref, *, q_blocks):
    z = _dot(x_ref[...], w_ref[...])
    scale = jnp.where(pl.program_id(0) >= q_blocks, DK_RET ** -0.5, 1.0).astype(F32)
    cos = cos_ref[...]
    sin = sin_ref[...]
    half = DK_RET // 2
    for h in range(z.shape[1] // DK_RET):
        x1 = z[:, h * DK_RET:h * DK_RET + half]
        x2 = z[:, h * DK_RET + half:(h + 1) * DK_RET]
        o_ref[:, h * DK_RET:h * DK_RET + half] = ((x1 * cos - x2 * sin) * scale).astype(o_ref.dtype)
        o_ref[:, h * DK_RET + half:(h + 1) * DK_RET] = ((x1 * sin + x2 * cos) * scale).astype(o_ref.dtype)


def _proj_rotary(xn, w, cos, sin, tm, tn):
    t, d = xn.shape
    n = w.shape[1]
    pos_blocks = cos.shape[0] // tm
    return pl.pallas_call(
        functools.partial(_proj_rotary_body, q_blocks=(n // 2) // tn),
        grid=(n // tn, t // tm),
        in_specs=[pl.BlockSpec((tm, d), lambda j, i: (i, 0)),
                  pl.BlockSpec((d, tn), lambda j, i: (0, j)),
                  pl.BlockSpec((tm, DK_RET // 2), lambda j, i: (i % pos_blocks, 0)),
                  pl.BlockSpec((tm, DK_RET // 2), lambda j, i: (i % pos_blocks, 0))],
        out_specs=pl.BlockSpec((tm, tn), lambda j, i: (i, j)),
        out_shape=jax.ShapeDtypeStruct((t, n), BF16),
        compiler_params=_cparams(("parallel", "parallel")),
        name="proj_rotary",
    )(xn, w, cos, sin)


def _proj_plain_body(x_ref, w_ref, o_ref):
    o_ref[...] = _dot(x_ref[...], w_ref[...]).astype(o_ref.dtype)


def _proj_sigmoid_body(x_ref, w_ref, o_ref):
    z = _dot(x_ref[...], w_ref[...])
    o_ref[...] = (1.0 / (1.0 + jnp.exp(-z))).astype(o_ref.dtype)


def _proj_headnorm_body(x_ref, w_ref, g_ref, o_ref):
    z = _dot(x_ref[...], w_ref[...])
    g = g_ref[...] * (D_HEAD ** -0.5 * LOG2_E)
    for h in range(z.shape[1] // D_HEAD):
        o_ref[:, h * D_HEAD:(h + 1) * D_HEAD] = (_rms(z[:, h * D_HEAD:(h + 1) * D_HEAD]) * g).astype(o_ref.dtype)


def _proj_simple(body, xn, w, tm, tn, name, extra=()):
    t, d = xn.shape
    n = w.shape[1]
    extra_specs = [pl.BlockSpec(e.shape, lambda j, i: (0, 0)) for e in extra]
    return pl.pallas_call(
        body,
        grid=(n // tn, t // tm),
        in_specs=[pl.BlockSpec((tm, d), lambda j, i: (i, 0)),
                  pl.BlockSpec((d, tn), lambda j, i: (0, j))] + extra_specs,
        out_specs=pl.BlockSpec((tm, tn), lambda j, i: (i, j)),
        out_shape=jax.ShapeDtypeStruct((t, n), BF16),
        compiler_params=_cparams(("parallel", "parallel")),
        name=name,
    )(xn, w, *extra)


_KV_W = N_KV_HEADS * D_HEAD
_IQ_W = N_IDX_HEADS * D_IDX
_S_K = 0
_S_V = _KV_W
_S_IQ = 2 * _KV_W
_S_IK = _S_IQ + _IQ_W
_S_IW = _S_IK + LANES
_S_END = _S_IW + LANES


def _proj_small_body(x_ref, w_ref, kg_ref, lnw_ref, lnb_ref,
                     k_ref, v_ref, kb_ref, vb_ref, iq_ref, ik_ref, ikb_ref, iw_ref):
    z = _dot(x_ref[...], w_ref[...])
    kg = kg_ref[...]
    for h in range(N_KV_HEADS):
        kh = _rms(z[:, _S_K + h * D_HEAD:_S_K + (h + 1) * D_HEAD]) * kg
        k_ref[:, h * D_HEAD:(h + 1) * D_HEAD] = kh
        kb_ref[:, h * D_HEAD:(h + 1) * D_HEAD] = kh.astype(BF16)
    v = z[:, _S_V:_S_V + _KV_W]
    v_ref[...] = v
    for h in range(N_KV_HEADS):
        vb_ref[:, 2 * h * D_HEAD:(2 * h + 1) * D_HEAD] = v[:, h * D_HEAD:(h + 1) * D_HEAD].astype(BF16)
        vb_ref[:, (2 * h + 1) * D_HEAD:(2 * h + 2) * D_HEAD] = jnp.ones((v.shape[0], D_HEAD), BF16)
    for h in range(N_IDX_HEADS):
        iq_ref[h] = z[:, _S_IQ + h * D_IDX:_S_IQ + (h + 1) * D_IDX].astype(BF16)
    ik = z[:, _S_IK:_S_IK + D_IDX]
    mu = jnp.mean(ik, axis=-1, keepdims=True)
    var = jnp.mean(jnp.square(ik - mu), axis=-1, keepdims=True)
    ikn = (ik - mu) * lax.rsqrt(var + NORM_EPS) * lnw_ref[...] + lnb_ref[...]
    ik_ref[...] = ikn
    ikb_ref[...] = ikn.astype(BF16)
    iw_ref[...] = z[:, _S_IW:_S_IW + N_IDX_HEADS] * IDX_SCALE


def _proj_small(xn, w, k_gain, ln_w, ln_b, tm):
    t, d = xn.shape
    row = lambda width: pl.BlockSpec((tm, width), lambda i: (i, 0))
    full = lambda a: pl.BlockSpec(a.shape, lambda i: (0, 0))
    return pl.pallas_call(
        _proj_small_body,
        grid=(t // tm,),
        in_specs=[row(d), full(w), full(k_gain), full(ln_w), full(ln_b)],
        out_specs=[row(_KV_W), row(_KV_W), row(_KV_W), row(2 * _KV_W),
                   pl.BlockSpec((N_IDX_HEADS, tm, D_IDX), lambda i: (0, i, 0)),
                   row(D_IDX), row(D_IDX), row(N_IDX_HEADS)],
        out_shape=[jax.ShapeDtypeStruct((t, _KV_W), F32), jax.ShapeDtypeStruct((t, _KV_W), F32),
                   jax.ShapeDtypeStruct((t, _KV_W), BF16), jax.ShapeDtypeStruct((t, 2 * _KV_W), BF16),
                   jax.ShapeDtypeStruct((N_IDX_HEADS, t, D_IDX), BF16),
                   jax.ShapeDtypeStruct((t, D_IDX), F32), jax.ShapeDtypeStruct((t, D_IDX), BF16),
                   jax.ShapeDtypeStruct((t, N_IDX_HEADS), F32)],
        compiler_params=_cparams(("parallel",)),
        name="proj_small",
    )(xn, w, k_gain, ln_w, ln_b)


def _retention_body(q_ref, k_ref, v_ref, s_in_ref, dec_ref, qd_ref, kd_ref, sd_ref,
                    o_ref, s_out_ref, s_scr, *, hb):
    c = pl.program_id(2)

    @pl.when(c == 0)
    def _():
        s_scr[...] = s_in_ref[0]

    for h in range(hb):
        q = q_ref[0, :, h * DK_RET:(h + 1) * DK_RET]
        k = k_ref[0, :, h * DK_RET:(h + 1) * DK_RET]
        v = v_ref[0, :, h * DV_RET:(h + 1) * DV_RET]
        s = s_scr[h]
        scores = _dot_nt(q, k) * dec_ref[h]
        o = _dot(scores.astype(BF16), v) + _dot(q, s.astype(BF16)) * qd_ref[h]
        kd = (k.astype(F32) * kd_ref[h]).astype(BF16)
        s_scr[h] = s * sd_ref[h] + _dot_tn(kd, v)
        o_ref[0, :, h * DV_RET:(h + 1) * DV_RET] = _rms(o).astype(o_ref.dtype)

    @pl.when(c == pl.num_programs(2) - 1)
    def _():
        s_out_ref[0] = s_scr[...]


def _retention_tables(n_heads, c):
    lg = jnp.log1p(-jnp.exp2(-5.0 - jnp.arange(n_heads, dtype=F32)))
    i = jnp.arange(c, dtype=F32)
    diff = i[:, None] - i[None, :]
    decay = jnp.where(diff >= 0, jnp.exp(lg[:, None, None] * jnp.maximum(diff, 0.0)), 0.0)
    q_decay = jnp.exp(lg[:, None] * (i + 1.0))
    k_decay = jnp.exp(lg[:, None] * (c - 1.0 - i))
    s_decay = jnp.exp(lg * c)
    return decay, q_decay[:, :, None], k_decay[:, :, None], s_decay[:, None, None]


def _retention(qk, vg, s_in, chunk, hb):
    b, l, _ = qk.shape
    n_heads = s_in.shape[1]
    nhb = n_heads // hb
    per_batch_state = s_in.shape[0] != 1
    tables = _retention_tables(n_heads, chunk)
    return pl.pallas_call(
        functools.partial(_retention_body, hb=hb),
        grid=(b, nhb, l // chunk),
        in_specs=[pl.BlockSpec((1, chunk, hb * DK_RET), lambda bi, hi, ci: (bi, ci, hi)),
                  pl.BlockSpec((1, chunk, hb * DK_RET), lambda bi, hi, ci: (bi, ci, nhb + hi)),
                  pl.BlockSpec((1, chunk, hb * DV_RET), lambda bi, hi, ci: (bi, ci, hi)),
                  pl.BlockSpec((1, hb, DK_RET, DV_RET),
                               (lambda bi, hi, ci: (bi, hi, 0, 0)) if per_batch_state
                               else (lambda bi, hi, ci: (0, hi, 0, 0))),
                  pl.BlockSpec((hb, chunk, chunk), lambda bi, hi, ci: (hi, 0, 0)),
                  pl.BlockSpec((hb, chunk, 1), lambda bi, hi, ci: (hi, 0, 0)),
                  pl.BlockSpec((hb, chunk, 1), lambda bi, hi, ci: (hi, 0, 0)),
                  pl.BlockSpec((hb, 1, 1), lambda bi, hi, ci: (hi, 0, 0))],
        out_specs=[pl.BlockSpec((1, chunk, hb * DV_RET), lambda bi, hi, ci: (bi, ci, hi)),
                   pl.BlockSpec((1, hb, DK_RET, DV_RET), lambda bi, hi, ci: (bi, hi, 0, 0))],
        out_shape=[jax.ShapeDtypeStruct((b, l, n_heads * DV_RET), BF16),
                   jax.ShapeDtypeStruct((b, n_heads, DK_RET, DV_RET), F32)],
        scratch_shapes=[pltpu.VMEM((hb, DK_RET, DV_RET), F32)],
        compiler_params=_cparams(("parallel", "parallel", "arbitrary")),
        name="retention",
    )(qk, qk, vg, s_in, *tables)


def _float_key(x):
    b = lax.bitcast_convert_type(x + 0.0, I32)
    return jnp.where(b < 0, b ^ jnp.int32(0x7FFFFFFF), b)


class _FlatKeys:
    def __init__(self, key_refs, mask_refs):
        self.key_refs, self.mask_refs = key_refs, mask_refs
        self.n, self.rows = len(key_refs), key_refs[0].shape[0]

    def fold(self, fn, carries, store=False):
        carries = list(carries)
        for c in range(self.key_refs[0].shape[1] // LANES):
            sl = slice(c * LANES, (c + 1) * LANES)
            for i, ref in enumerate(self.key_refs):
                out = fn(i, ref[:, sl], carries[i])
                if store:
                    self.mask_refs[i][:, sl] = out[0]
                carries[i] = out[1] if store else out
        return carries


def _select_topk(keys, k):
    zeros = [jnp.zeros((keys.rows, LANES), F32)] * keys.n

    def counts(cands):
        accs = keys.fold(lambda i, piece, acc: acc + jnp.where(piece >= cands[i], 1.0, 0.0), zeros)
        return [jnp.sum(a, axis=-1, keepdims=True) for a in accs]

    zero = jnp.zeros((keys.rows, 1), I32)
    bases = tuple(jnp.where(c >= k, zero, jnp.full_like(zero, -2 ** 31)) for c in counts([zero] * keys.n))

    def descend(i, bases):
        bit = jnp.left_shift(jnp.int32(1), 30 - i)
        cands = [b | bit for b in bases]
        return tuple(jnp.where(c >= k, cand, b) for c, cand, b in zip(counts(cands), cands, bases))

    thrs = [jnp.maximum(t, KEY_NEG_INF + 1) for t in lax.fori_loop(0, 31, descend, bases)]
    needs = [k - c for c in counts([t + 1 for t in thrs])]

    r_i = lax.broadcasted_iota(I32, (LANES, 2 * LANES), 0)
    c_i = lax.broadcasted_iota(I32, (LANES, 2 * LANES), 1)
    tri = jnp.where((c_i >= LANES) | (r_i <= c_i), 1.0, 0.0).astype(BF16)

    def mask_piece(i, piece, seen):
        eq = piece == thrs[i]
        pre = _dot(jnp.where(eq, 1.0, 0.0).astype(BF16), tri)
        keep_eq = jnp.where(pre[:, :LANES] + seen <= needs[i], 0.0, -jnp.inf)
        mask = jnp.where(piece > thrs[i], 0.0, jnp.where(eq, keep_eq, -jnp.inf))
        return mask, seen + pre[:, LANES:]

    keys.fold(mask_piece, zeros, store=True)


def _indexer_keys(iq_stack, iw_cols, ik, valid, keys_transposed=False):
    r = valid.shape[0]
    s = _dot(iq_stack, ik) if keys_transposed else _dot_nt(iq_stack, ik)
    acc = None
    for h in range(N_IDX_HEADS):
        t = jnp.maximum(s[h * r:(h + 1) * r], 0.0) * iw_cols[h]
        acc = t if acc is None else acc + t
    return jnp.where(valid, _float_key(acc), KEY_NEG_INF)


def _softmax_pv(s_chunks, v_chunks):
    m = functools.reduce(jnp.maximum, [jnp.max(sc, axis=-1, keepdims=True) for sc in s_chunks])
    acc = functools.reduce(jnp.add, [_dot(jnp.exp2((sc - m).astype(BF16)), vc)
                                     for sc, vc in zip(s_chunks, v_chunks)])
    return acc[:, :D_HEAD] / acc[:, D_HEAD:D_HEAD + 1]


def _dsa_prompt_group(aq_ref, iq_ref, iw_ref, k_ref, v_ref, ik_ref, mk_ref, mv_ref, mik_ref,
                      o_ref, key_scr, mask_scr, *, topk, key_chunk, sub, w):
    rows = aq_ref.shape[1]
    n_sub = rows // sub
    wm = mk_ref.shape[0]
    n_heads = aq_ref.shape[2] // D_HEAD
    group = n_heads // N_KV_HEADS
    row0 = pl.program_id(1) * rows

    def index_block(i, carry):
        r0 = pl.multiple_of(i * sub, sub)
        rs = pl.ds(r0, sub)
        qpos = N_META + row0 + r0 + lax.broadcasted_iota(I32, (sub, 1), 0)
        iq_stack = iq_ref[:, rs, :].reshape(N_IDX_HEADS * sub, D_IDX)
        iw = iw_ref[0, rs, :]
        iw_cols = [iw[:, h:h + 1] for h in range(N_IDX_HEADS)]
        mcol = lax.broadcasted_iota(I32, (1, wm), 1)
        key_scr[rs, :wm] = _indexer_keys(iq_stack, iw_cols, mik_ref[...],
                                         jnp.broadcast_to(mcol < N_META, (sub, wm)))
        for c0 in range(0, w, key_chunk):
            pos = N_META + c0 + lax.broadcasted_iota(I32, (1, key_chunk), 1)
            key_scr[rs, wm + c0:wm + c0 + key_chunk] = _indexer_keys(
                iq_stack, iw_cols, ik_ref[0, c0:c0 + key_chunk, :], pos <= qpos)
        return carry

    lax.fori_loop(0, n_sub, index_block, 0)

    blocks = [slice(i * sub, (i + 1) * sub) for i in range(n_sub)]
    _select_topk(_FlatKeys([key_scr.at[bs, :wm + w] for bs in blocks],
                           [mask_scr.at[bs, :wm + w] for bs in blocks]), topk)

    def attend_block(i, carry):
        rs = pl.ds(pl.multiple_of(i * sub, sub), sub)
        mask_meta = mask_scr[rs, :wm]
        mask_real = mask_scr[rs, wm:wm + w]
        for n in range(N_KV_HEADS):
            hs = slice(n * D_HEAD, (n + 1) * D_HEAD)
            qs = jnp.concatenate([aq_ref[0, rs, (n * group + g) * D_HEAD:(n * group + g + 1) * D_HEAD]
                                  for g in range(group)], axis=0)
            s_real = (_dot_nt(qs, k_ref[0, :w, hs]).reshape(group, sub, w) + mask_real[None])
            s_meta = (_dot_nt(qs, mk_ref[:, hs]).reshape(group, sub, wm) + mask_meta[None])
            vs = slice(2 * n * D_HEAD, (2 * n + 2) * D_HEAD)
            o = _softmax_pv([s_real.reshape(group * sub, w), s_meta.reshape(group * sub, wm)],
                            [v_ref[0, :w, vs], mv_ref[:, vs]])
            for g in range(group):
                o_ref[0, rs, (n * group + g) * D_HEAD:(n * group + g + 1) * D_HEAD] = (
                    o[g * sub:(g + 1) * sub].astype(o_ref.dtype))
        return carry

    lax.fori_loop(0, n_sub, attend_block, 0)


def _dsa_prompt_body(*refs, topk, key_chunk, sub):
    rows = refs[0].shape[1]
    for v in range(refs[3].shape[1] // rows):
        pl.when(pl.program_id(1) == v)(functools.partial(
            _dsa_prompt_group, *refs, topk=topk, key_chunk=min(key_chunk, rows), sub=sub, w=(v + 1) * rows))


def _dsa_prompt(aq, iq_hm, iw, kb, vb, ikb, mk, mv, mik, topk):
    b, s, dq = aq.shape
    sub = min(Q_BLOCK, s)
    rows = 4 * sub if s % (4 * sub) == 0 else sub
    ng = s // rows
    wm = mk.shape[0]
    return pl.pallas_call(
        functools.partial(_dsa_prompt_body, topk=topk, key_chunk=512, sub=sub),
        grid=(b, ng),
        in_specs=[pl.BlockSpec((1, rows, dq), lambda bi, j: (bi, j, 0)),
                  pl.BlockSpec((N_IDX_HEADS, rows, D_IDX), lambda bi, j: (0, bi * ng + j, 0)),
                  pl.BlockSpec((1, rows, N_IDX_HEADS), lambda bi, j: (bi, j, 0)),
                  pl.BlockSpec((1, s, _KV_W), lambda bi, j: (bi, 0, 0)),
                  pl.BlockSpec((1, s, 2 * _KV_W), lambda bi, j: (bi, 0, 0)),
                  pl.BlockSpec((1, s, D_IDX), lambda bi, j: (bi, 0, 0)),
                  pl.BlockSpec(mk.shape, lambda bi, j: (0, 0)),
                  pl.BlockSpec(mv.shape, lambda bi, j: (0, 0)),
                  pl.BlockSpec(mik.shape, lambda bi, j: (0, 0))],
        out_specs=pl.BlockSpec((1, rows, dq), lambda bi, j: (bi, j, 0)),
        out_shape=jax.ShapeDtypeStruct((b, s, dq), BF16),
        scratch_shapes=[pltpu.VMEM((rows, s + wm), I32), pltpu.VMEM((rows, s + wm), F32)],
        compiler_params=_cparams(("parallel", "arbitrary")),
        name="dsa_prompt",
    )(aq, iq_hm, iw, kb, vb, ikb, mk, mv, mik)


def _page_specs(n_pages, rows, width):
    return [pl.BlockSpec((1, rows, width), functools.partial(lambda bi, pt, p: (pt[bi, p], 0, 0), p=p))
            for p in range(n_pages)]


def _sample_keys_body(pt_ref, iq_ref, iw_ref, ikn_ref, *rest, n_pages):
    pages = rest[:n_pages]
    key_ref = rest[n_pages]
    nq = key_ref.shape[1]
    iq_stack = iq_ref[0].reshape(N_IDX_HEADS * nq, D_IDX).astype(BF16)
    iw = iw_ref[0]
    iw_cols = [iw[h * nq:(h + 1) * nq] for h in range(N_IDX_HEADS)]
    all_valid = jnp.full((nq, PAGE_SIZE), True)
    for p in range(n_pages):
        key_ref[0, :, p * PAGE_SIZE:(p + 1) * PAGE_SIZE] = _indexer_keys(
            iq_stack, iw_cols, pages[p][0].astype(BF16), all_valid, keys_transposed=True)
    wn = ikn_ref.shape[1]
    col = lax.broadcasted_iota(I32, (nq, wn), 1)
    row = lax.broadcasted_iota(I32, (nq, wn), 0)
    key_ref[0, :, n_pages * PAGE_SIZE:n_pages * PAGE_SIZE + wn] = _indexer_keys(
        iq_stack, iw_cols, ikn_ref[0], col <= row)


def _sample_keys(page_table, iq4, iw_col, ik_new, cache_kidx):
    db, _, nq, _ = iq4.shape
    n_pages = page_table.shape[1]
    wn = ik_new.shape[1]
    wt = n_pages * PAGE_SIZE + wn
    grid_spec = pltpu.PrefetchScalarGridSpec(
        num_scalar_prefetch=1,
        grid=(db,),
        in_specs=[pl.BlockSpec((1, N_IDX_HEADS, nq, D_IDX), lambda bi, pt: (bi, 0, 0, 0)),
                  pl.BlockSpec((1, N_IDX_HEADS * nq, 1), lambda bi, pt: (bi, 0, 0)),
                  pl.BlockSpec((1, wn, D_IDX), lambda bi, pt: (bi, 0, 0))]
        + _page_specs(n_pages, D_IDX, PAGE_SIZE),
        out_specs=pl.BlockSpec((1, nq, wt), lambda bi, pt: (bi, 0, 0)),
    )
    return pl.pallas_call(
        functools.partial(_sample_keys_body, n_pages=n_pages),
        grid_spec=grid_spec,
        out_shape=jax.ShapeDtypeStruct((db, nq, wt), I32),
        compiler_params=_cparams(("arbitrary",)),
        name="sample_keys",
    )(page_table, iq4, iw_col, ik_new, *([cache_kidx] * n_pages))


def _select_body(key_ref, mask_ref, *, topk):
    rows = key_ref.shape[0]
    groups = [slice(r0, min(r0 + Q_BLOCK, rows)) for r0 in range(0, rows, Q_BLOCK)]
    _select_topk(_FlatKeys([key_ref.at[g] for g in groups], [mask_ref.at[g] for g in groups]), topk)


def _select_mask(keys, topk, tr):
    rows, wt = keys.shape
    return pl.pallas_call(
        functools.partial(_select_body, topk=topk),
        grid=(rows // tr,),
        in_specs=[pl.BlockSpec((tr, wt), lambda i: (i, 0))],
        out_specs=pl.BlockSpec((tr, wt), lambda i: (i, 0)),
        out_shape=jax.ShapeDtypeStruct((rows, wt), F32),
        compiler_params=_cparams(("parallel",)),
        name="select_mask",
    )(keys)


def _sample_attn_body(pt_ref, q_ref, mask_ref, kn_ref, vn_ref, *rest, n_pages):
    kpages = rest[:n_pages]
    vpages = rest[n_pages:2 * n_pages]
    o_ref = rest[2 * n_pages]
    n_heads, nq = q_ref.shape[1], q_ref.shape[2]
    group = n_heads // N_KV_HEADS
    q = q_ref[0].reshape(n_heads * nq, D_HEAD).astype(BF16)
    wn = kn_ref.shape[1]
    chunks = [(p * PAGE_SIZE, PAGE_SIZE) for p in range(n_pages)] + [(n_pages * PAGE_SIZE, wn)]
    for n in range(N_KV_HEADS):
        qn = q[n * group * nq:(n + 1) * group * nq]
        hs = slice(n * D_HEAD, (n + 1) * D_HEAD)
        head_rows = pl.ds(n, PAGE_SIZE, stride=N_KV_HEADS)
        ks = [kpages[p][0, head_rows, :].astype(BF16) for p in range(n_pages)] + [kn_ref[0, :, hs]]
        vs = [vpages[p][0, head_rows, :].astype(BF16) for p in range(n_pages)] + [
            vn_ref[0, :, 2 * n * D_HEAD:(2 * n + 1) * D_HEAD]]
        scores = []
        for (c0, cw), kc in zip(chunks, ks):
            mk = mask_ref[0, :, c0:c0 + cw]
            scores.append((_dot_nt(qn, kc).reshape(group, nq, cw) + mk[None]).reshape(group * nq, cw))
        m = functools.reduce(jnp.maximum, [jnp.max(sc, axis=-1, keepdims=True) for sc in scores])
        ps = [jnp.exp2(sc - m) for sc in scores]
        denom = functools.reduce(jnp.add, [jnp.sum(pc, axis=-1, keepdims=True) for pc in ps])
        o = functools.reduce(jnp.add, [_dot(pc.astype(BF16), vc) for pc, vc in zip(ps, vs)])
        o_ref[0, n * group:(n + 1) * group] = (o / denom).reshape(group, nq, D_HEAD)


def _sample_attn(page_table, q4, mask, k_new, v_new, cache_k, cache_v):
    db, n_heads, nq, _ = q4.shape
    n_pages = page_table.shape[1]
    wn = k_new.shape[1]
    wt = mask.shape[2]
    grid_spec = pltpu.PrefetchScalarGridSpec(
        num_scalar_prefetch=1,
        grid=(db,),
        in_specs=[pl.BlockSpec((1, n_heads, nq, D_HEAD), lambda bi, pt: (bi, 0, 0, 0)),
                  pl.BlockSpec((1, nq, wt), lambda bi, pt: (bi, 0, 0)),
                  pl.BlockSpec((1, wn, _KV_W), lambda bi, pt: (bi, 0, 0)),
                  pl.BlockSpec((1, wn, 2 * _KV_W), lambda bi, pt: (bi, 0, 0))]
        + 2 * _page_specs(n_pages, PAGE_SIZE * N_KV_HEADS, D_HEAD),
        out_specs=pl.BlockSpec((1, n_heads, nq, D_HEAD), lambda bi, pt: (bi, 0, 0, 0)),
    )
    return pl.pallas_call(
        functools.partial(_sample_attn_body, n_pages=n_pages),
        grid_spec=grid_spec,
        out_shape=jax.ShapeDtypeStruct((db, n_heads, nq, D_HEAD), F32),
        compiler_params=_cparams(("arbitrary",)),
        name="sample_attn",
    )(page_table, q4, mask, k_new, v_new, *([cache_k] * n_pages), *([cache_v] * n_pages))


def _merge_body(oret_ref, rg_ref, odsa_ref, gates_ref, h_ref, wr_ref, wd_ref, wo_ref, nw_ref,
                h1_ref, hn_ref):
    d = h_ref.shape[1]
    rg = rg_ref[...].astype(F32)
    a = oret_ref[...].astype(F32) * (rg / (1.0 + jnp.exp(-rg)))
    br_ret = _dot(a.astype(BF16), wr_ref[...])
    br_dsa = _dot(odsa_ref[...], wd_ref[...])
    mix = gates_ref[:, :d].astype(F32) * br_ret + gates_ref[:, d:].astype(F32) * br_dsa
    h1 = h_ref[...] + _dot(mix.astype(BF16), wo_ref[...])
    h1_ref[...] = h1
    hn_ref[...] = (_rms(h1) * nw_ref[...]).astype(hn_ref.dtype)


def _merge(oret, vg, odsa, gates, h, w_ret, w_dsa, w_out, norm_w, tm):
    t, d = h.shape
    rw = oret.shape[1]
    resident = lambda a: pl.BlockSpec(a.shape, lambda i: (0, 0))
    return pl.pallas_call(
        _merge_body,
        grid=(t // tm,),
        in_specs=[pl.BlockSpec((tm, rw), lambda i: (i, 0)),
                  pl.BlockSpec((tm, rw), lambda i: (i, 1)),
                  pl.BlockSpec((tm, d), lambda i: (i, 0)),
                  pl.BlockSpec((tm, 2 * d), lambda i: (i, 0)),
                  pl.BlockSpec((tm, d), lambda i: (i, 0)),
                  resident(w_ret), resident(w_dsa), resident(w_out), resident(norm_w)],
        out_specs=[pl.BlockSpec((tm, d), lambda i: (i, 0)), pl.BlockSpec((tm, d), lambda i: (i, 0))],
        out_shape=[jax.ShapeDtypeStruct((t, d), F32), jax.ShapeDtypeStruct((t, d), BF16)],
        compiler_params=_cparams(("parallel",)),
        name="merge",
    )(oret, vg, odsa, gates, h, w_ret, w_dsa, w_out, norm_w)


def _ffn_body(hn_ref, h1_ref, wa_ref, wb_ref, wo_ref, y_ref, acc_ref):
    f = pl.program_id(1)
    a = _dot(hn_ref[...], wa_ref[...])
    b = _dot(hn_ref[...], wb_ref[...])
    g = (a / (1.0 + jnp.exp(-a)) * b).astype(BF16)
    part = _dot(g, wo_ref[...])

    @pl.when(f == 0)
    def _():
        acc_ref[...] = h1_ref[...] + part

    @pl.when(f != 0)
    def _():
        acc_ref[...] = acc_ref[...] + part

    @pl.when(f == pl.num_programs(1) - 1)
    def _():
        y_ref[...] = acc_ref[...]


def _ffn(hn, h1, w_in, w_out, tm, tf):
    t, d = h1.shape
    dff = w_out.shape[0]
    nf = dff // tf
    return pl.pallas_call(
        _ffn_body,
        grid=(t // tm, nf),
        in_specs=[pl.BlockSpec((tm, d), lambda i, f: (i, 0)),
                  pl.BlockSpec((tm, d), lambda i, f: (i, 0)),
                  pl.BlockSpec((d, tf), lambda i, f: (0, f)),
                  pl.BlockSpec((d, tf), lambda i, f: (0, nf + f)),
                  pl.BlockSpec((tf, d), lambda i, f: (f, 0))],
        out_specs=pl.BlockSpec((tm, d), lambda i, f: (i, 0)),
        out_shape=jax.ShapeDtypeStruct((t, d), F32),
        scratch_shapes=[pltpu.VMEM((tm, d), F32)],
        compiler_params=_cparams(("parallel", "arbitrary")),
        name="ffn",
    )(hn, h1, w_in, w_in, w_out)


def _rope_tables(pos):
    half = DK_RET // 2
    inv = ROPE_BASE ** (-jnp.arange(half, dtype=F32) / half)
    ang = pos.astype(F32)[:, None] * inv[None, :]
    return jnp.cos(ang), jnp.sin(ang)


def _row_tile(t, cap):
    tm = min(t, cap)
    assert t % tm == 0
    return tm


def _pad_rows(a, rows):
    return jnp.pad(a, [(0, rows - a.shape[0])] + [(0, 0)] * (a.ndim - 1))


def kernel(x_prompt, x_sample, cache_k, cache_v, cache_kidx, state_ret, page_table, meta_tokens,
           norm_mix_w, w_in, w_ret_proj, dsa_q_norm_w, dsa_k_norm_w, idx_k_norm_w, idx_k_norm_b,
           w_dsa_proj, w_out, norm_ffn_w, w_ffn_in, w_ffn_out):
    assert w_in.shape[0] == 1, "single-layer trunk"
    b, seq, d = x_prompt.shape
    db, dseq, _ = x_sample.shape
    n_meta = meta_tokens.shape[0]
    assert n_meta == N_META
    h_ret = d // DK_RET
    n_heads = d // D_HEAD
    rqk_w, rv_w = h_ret * DK_RET, h_ret * DV_RET
    n_pages = page_table.shape[1]
    past = n_pages * PAGE_SIZE
    n_pool = cache_k.shape[1]

    wi = w_in[0]
    c0 = 0
    w_rqk = wi[:, c0:c0 + 2 * rqk_w].astype(BF16); c0 += 2 * rqk_w
    w_rvg = wi[:, c0:c0 + 2 * rv_w].astype(BF16); c0 += 2 * rv_w
    w_aq = wi[:, c0:c0 + n_heads * D_HEAD].astype(BF16); c0 += n_heads * D_HEAD
    w_ak = wi[:, c0:c0 + _KV_W]; c0 += _KV_W
    w_av = wi[:, c0:c0 + _KV_W]; c0 += _KV_W
    w_iq = wi[:, c0:c0 + _IQ_W]; c0 += _IQ_W
    w_ik = wi[:, c0:c0 + D_IDX]; c0 += D_IDX
    w_iw = wi[:, c0:c0 + N_IDX_HEADS]; c0 += N_IDX_HEADS
    w_gz = wi[:, c0:c0 + 2 * d].astype(BF16); c0 += 2 * d
    assert c0 == wi.shape[1]
    zpad = lambda n: jnp.zeros((d, n), F32)
    w_small = jnp.concatenate([w_ak, w_av, w_iq, w_ik, zpad(LANES - D_IDX), w_iw, zpad(LANES - N_IDX_HEADS)],
                              axis=1).astype(BF16)
    assert w_small.shape[1] == _S_END
    q_gain = dsa_q_norm_w[0].reshape(1, D_HEAD)
    k_gain = dsa_k_norm_w[0].reshape(1, D_HEAD)
    ln_w = idx_k_norm_w[0].reshape(1, D_IDX)
    ln_b = idx_k_norm_b[0].reshape(1, D_IDX)
    w_ret_b = w_ret_proj[0].astype(BF16)
    w_dsa_b = w_dsa_proj[0].astype(BF16)
    w_out_b = w_out[0].astype(BF16)
    w_ffn_in_b = w_ffn_in[0].astype(BF16)
    w_ffn_out_b = w_ffn_out[0].astype(BF16)
    ffn_gain = norm_ffn_w[0].reshape(1, d)

    def project(x2d, pos, tm, want_queries):
        xn = _rmsnorm_bf16(x2d, norm_mix_w[0], tm)
        cos, sin = _rope_tables(pos)
        reps = max(1, tm // pos.shape[0])
        cos, sin = jnp.tile(cos, (reps, 1)), jnp.tile(sin, (reps, 1))
        out = {}
        out["rqk"] = _proj_rotary(xn, w_rqk, cos, sin, tm, rqk_w)
        out["rvg"] = _proj_simple(_proj_plain_body, xn, w_rvg, tm, 1024, "proj_rvg")
        (out["k"], out["v"], out["kb"], out["vb"], out["iq"], out["ik"], out["ikb"], out["iw"]) = _proj_small(
            xn, w_small, k_gain, ln_w, ln_b, tm)
        if want_queries:
            out["aq"] = _proj_simple(_proj_headnorm_body, xn, w_aq, tm, n_heads * D_HEAD, "proj_aq", (q_gain,))
            out["gates"] = _proj_simple(_proj_sigmoid_body, xn, w_gz, tm, 1024, "proj_gates")
        return out

    def post(x2d, oret, rvg, odsa, gates, tm):
        h1, hn = _merge(oret, rvg, odsa, gates, x2d, w_ret_b, w_dsa_b, w_out_b, ffn_gain, tm)
        dff = w_ffn_out_b.shape[0]
        tf = dff // 2 if (dff // 2) % LANES == 0 else dff
        return _ffn(hn, h1, w_ffn_in_b, w_ffn_out_b, tm, tf)

    pm = project(meta_tokens, jnp.arange(n_meta), n_meta, False)
    s_zero = jnp.zeros((1, h_ret, DK_RET, DV_RET), F32)
    _, s_meta = _retention(pm["rqk"][None], pm["rvg"][None], s_zero, n_meta, 1)

    tp = b * seq
    xp = x_prompt.reshape(tp, d)
    tm_p = _row_tile(seq, 512)
    pp = project(xp, n_meta + jnp.arange(seq), tm_p, True)
    oret_p, s_prompt = _retention(pp["rqk"].reshape(b, seq, -1), pp["rvg"].reshape(b, seq, -1),
                                  s_meta, RET_CHUNK, h_ret)
    topk_p = min(INDEX_TOPK, seq // 4)
    odsa_p = _dsa_prompt(pp["aq"].reshape(b, seq, -1), pp["iq"], pp["iw"].reshape(b, seq, -1),
                         pp["kb"].reshape(b, seq, -1), pp["vb"].reshape(b, seq, -1),
                         pp["ikb"].reshape(b, seq, -1),
                         _pad_rows(pm["kb"], LANES), _pad_rows(pm["vb"], LANES), _pad_rows(pm["ikb"], LANES),
                         topk_p)
    y_prompt = post(xp, oret_p.reshape(tp, -1), pp["rvg"], odsa_p.reshape(tp, -1), pp["gates"], tm_p)

    ts = db * dseq
    xs = x_sample.reshape(ts, d)
    tm_s = _row_tile(ts, 512)
    ps = project(xs, past + jnp.arange(dseq), tm_s, True)
    oret_s, s_sample = _retention(ps["rqk"].reshape(db, dseq, -1), ps["rvg"].reshape(db, dseq, -1),
                                  state_ret[0], dseq, h_ret)
    topk_s = min(INDEX_TOPK, (past + dseq) // 4)
    iq4 = ps["iq"].reshape(N_IDX_HEADS, db, dseq, D_IDX).transpose(1, 0, 2, 3).astype(F32)
    iw_col = ps["iw"].reshape(db, dseq, N_IDX_HEADS).transpose(0, 2, 1).reshape(db, N_IDX_HEADS * dseq, 1)
    pad_new = lambda a: jnp.pad(a.reshape(db, dseq, -1), ((0, 0), (0, LANES - dseq), (0, 0)))
    kidx_pages = jnp.swapaxes(cache_kidx[0], 1, 2)
    k_pages = cache_k[0].reshape(n_pool, PAGE_SIZE * N_KV_HEADS, D_HEAD)
    v_pages = cache_v[0].reshape(n_pool, PAGE_SIZE * N_KV_HEADS, D_HEAD)
    keys_s = _sample_keys(page_table, iq4, iw_col, pad_new(ps["ikb"]), kidx_pages)
    wt = keys_s.shape[2]
    mask_s = _select_mask(keys_s.reshape(ts, wt), topk_s, _row_tile(ts, 256)).reshape(db, dseq, wt)
    q4 = ps["aq"].reshape(db, dseq, n_heads, D_HEAD).transpose(0, 2, 1, 3).astype(F32)
    o4 = _sample_attn(page_table, q4, mask_s, pad_new(ps["kb"]), pad_new(ps["vb"]), k_pages, v_pages)
    odsa_s = o4.transpose(0, 2, 1, 3).reshape(ts, n_heads * D_HEAD).astype(BF16)
    y_sample = post(xs, oret_s.reshape(ts, -1), ps["rvg"], odsa_s, ps["gates"], tm_s)

    def with_meta(meta, real, tail):
        m = jnp.broadcast_to(meta[None], (b,) + meta.shape)
        return jnp.concatenate([m, real.reshape(b, seq, -1)], axis=1).reshape((1, b, n_meta + seq) + tail)

    kv_tail = (N_KV_HEADS, D_HEAD)
    return (y_prompt.reshape(b, seq, d),
            y_sample.reshape(db, dseq, d),
            with_meta(pm["k"], pp["k"], kv_tail),
            with_meta(pm["v"], pp["v"], kv_tail),
            with_meta(pm["ik"], pp["ik"], (D_IDX,)),
            s_prompt[None],
            ps["k"].reshape((1, db, dseq) + kv_tail),
            ps["v"].reshape((1, db, dseq) + kv_tail),
            ps["ik"].reshape(1, db, dseq, D_IDX),
            s_sample[None])
```
